```python
import math
import jax
import jax.numpy as jnp
from jax import lax
import numpy as np

D_MODEL = 1024
BATCH = 8
SEQ = 4096
DEPTH = 4

GRID_W = 64
CTX_LEN = 256
N_DIR = 2
N_BRANCH = 2
EPS = 1e-6
LRU_WIDTH = 1024
LRU_HEADS = 16
LRU_HEAD_DIM = LRU_WIDTH // LRU_HEADS
LRU_CONV = 4
LRU_C = 8.0
SSD_INNER = 2 * D_MODEL
SSD_HEAD_DIM = 64
SSD_HEADS = SSD_INNER // SSD_HEAD_DIM
SSD_GROUPS = 4
SSD_STATE = 128
SSD_CONV = 4
SSD_CHUNK = 128
SSD_CONV_DIM = SSD_INNER + 2 * SSD_GROUPS * SSD_STATE
FFN_DIM = 2816
FFN_CONV = 3
IN_SIZES = (LRU_WIDTH, LRU_WIDTH, SSD_INNER, SSD_CONV_DIM, N_DIR * SSD_HEADS, N_BRANCH * D_MODEL)
IN_DIM = 2 * LRU_WIDTH + SSD_INNER + SSD_CONV_DIM + N_DIR * SSD_HEADS + N_BRANCH * D_MODEL

kernel_name = "hybrid_rglru_ssd_convffn_dit"


def rmsnorm(x, w):
    xf = x.astype(jnp.float32)
    y = xf * lax.rsqrt(jnp.mean(xf * xf, axis=-1, keepdims=True) + EPS)
    return (y * w.astype(jnp.float32)).astype(x.dtype)


def modulate(h, shift, scale):
    return h * (1 + scale) + shift


def split_cols(t, sizes):
    offs, acc = [], 0
    for s in sizes[:-1]:
        acc += s
        offs.append(acc)
    return jnp.split(t, offs, axis=-1)


def rev(t, d):
    return t[:, ::-1] if d == 1 else t


def dwconv1d(x, w, b):
    k, ch = w.shape
    y = lax.conv_general_dilated(x, w[:, None, :].astype(x.dtype), window_strides=(1,),
                                 padding=((k // 2, k - 1 - k // 2),),
                                 dimension_numbers=('NWC', 'WIO', 'NWC'), feature_group_count=ch)
    return y + b


def dwconv2d(x, w, b, rows, cols):
    bsz, seqlen, ch = x.shape
    g = x.reshape(bsz, rows, cols, ch)
    y = lax.conv_general_dilated(g, w[:, :, None, :].astype(x.dtype), window_strides=(1, 1),
                                 padding=((1, 1), (1, 1)),
                                 dimension_numbers=('NHWC', 'HWIO', 'NHWC'), feature_group_count=ch)
    return y.reshape(bsz, seqlen, ch) + b


def linear_scan(a, b, h0):
    def comb(left, right):
        al, bl = left
        ar, br = right
        return ar * al, ar * bl + br
    a_cum, h = lax.associative_scan(comb, (a, b), axis=1)
    return h + a_cum * h0[:, None, :]


def rglru(u, wa, ba, wx, bx, lam, h0):
    bsz, seqlen, width = u.shape
    ub = u.reshape(bsz, seqlen, LRU_HEADS, LRU_HEAD_DIM)
    r = jax.nn.sigmoid(jnp.einsum('blhi,hij->blhj', ub, wa.astype(jnp.float32)).reshape(bsz, seqlen, width)
                       + ba.astype(jnp.float32))
    i = jax.nn.sigmoid(jnp.einsum('blhi,hij->blhj', ub, wx.astype(jnp.float32)).reshape(bsz, seqlen, width)
                       + bx.astype(jnp.float32))
    log_a = LRU_C * r * jax.nn.log_sigmoid(lam.astype(jnp.float32))
    a = jnp.exp(log_a)
    b = jnp.sqrt(-jnp.expm1(2.0 * log_a)) * (i * u)
    return linear_scan(a, b, h0)


def segsum(x):
    t = x.shape[-1]
    cs = jnp.cumsum(x, axis=-1)
    s = cs[..., :, None] - cs[..., None, :]
    mask = jnp.tril(jnp.ones((t, t), dtype=bool))
    return jnp.where(mask, s, -jnp.inf)


def ssd_chunked(xs, dt, a, bm, cm, h0):
    bsz, seqlen, nh, hp = xs.shape
    ng, ns = bm.shape[2], bm.shape[3]
    rep = nh // ng
    q = SSD_CHUNK
    nc = seqlen // q
    xdt = (xs * dt[..., None]).reshape(bsz, nc, q, ng, rep, hp)
    da = (dt * a).reshape(bsz, nc, q, ng, rep).transpose(0, 3, 4, 1, 2)
    da_cs = jnp.cumsum(da, axis=-1)
    bc = bm.reshape(bsz, nc, q, ng, ns)
    cc = cm.reshape(bsz, nc, q, ng, ns)
    decay_in = jnp.exp(segsum(da))
    cb = jnp.einsum('bclgn,bcsgn->bgcls', cc, bc)
    y_diag = jnp.einsum('bgcls,bgrcls,bcsgrp->bclgrp', cb, decay_in, xdt)
    decay_st = jnp.exp(da_cs[..., -1:] - da_cs)
    states = jnp.einsum('bcsgn,bgrcs,bcsgrp->bcgrpn', bc, decay_st, xdt)
    states = jnp.concatenate([h0.reshape(bsz, 1, ng, rep, hp, ns), states], axis=1)
    chunk_tot = jnp.pad(da_cs[..., -1], ((0, 0), (0, 0), (0, 0), (1, 0)))
    decay_ch = jnp.exp(segsum(chunk_tot))
    states = jnp.einsum('bgrzc,bcgrpn->bzgrpn', decay_ch, states)
    prev, final = states[:, :-1], states[:, -1]
    y_off = jnp.einsum('bclgn,bcgrpn,bgrcl->bclgrp', cc, prev, jnp.exp(da_cs))
    y = (y_diag + y_off).reshape(bsz, seqlen, nh, hp)
    return y, final.reshape(bsz, nh, hp, ns)


def ssd_final_state(xs, dt, a, bm):
    bsz, seqlen, nh, hp = xs.shape
    ng, ns = bm.shape[2], bm.shape[3]
    rep = nh // ng
    da_cs = jnp.cumsum(dt * a, axis=1)
    w = (jnp.exp(da_cs[:, -1:] - da_cs) * dt).reshape(bsz, seqlen, ng, rep)
    st = jnp.einsum('blgn,blgr,blgrp->bgrpn', bm, w, xs.reshape(bsz, seqlen, ng, rep, hp))
    return st.reshape(bsz, nh, hp, ns)


def ssd_heads(v):
    bsz, seqlen, _ = v.shape
    xs, bm, cm = split_cols(v.astype(jnp.float32), (SSD_INNER, SSD_GROUPS * SSD_STATE, SSD_GROUPS * SSD_STATE))
    return (xs.reshape(bsz, seqlen, SSD_HEADS, SSD_HEAD_DIM),
            bm.reshape(bsz, seqlen, SSD_GROUPS, SSD_STATE),
            cm.reshape(bsz, seqlen, SSD_GROUPS, SSD_STATE))


def ssd_output(y, xs, z, ssd_d, norm_w):
    bsz, seqlen = z.shape[:2]
    y = (y + ssd_d.astype(jnp.float32)[:, None] * xs).reshape(bsz, seqlen, SSD_INNER)
    g = (y * jax.nn.silu(z.astype(jnp.float32))).reshape(bsz, seqlen, SSD_GROUPS, SSD_INNER // SSD_GROUPS)
    g = g * lax.rsqrt(jnp.mean(g * g, axis=-1, keepdims=True) + EPS)
    return (g.reshape(bsz, seqlen, SSD_INNER) * norm_w.astype(jnp.float32)).astype(z.dtype)


def mixer(hc, hx, w_in, lru_conv_w, lru_conv_b, lru_wa, lru_ba, lru_wx, lru_bx, lru_lambda, lru_proj,
          ssd_conv_w, ssd_conv_b, ssd_dt_bias, ssd_a_log, ssd_d, ssd_norm_w, ssd_proj, w_out, ctx_out):
    lx_c, lg_c, z_c, xbc_c, dt_c, gt_c = split_cols(hc @ w_in, IN_SIZES)
    lx_x, lg_x, z_x, xbc_x, dt_x, gt_x = split_cols(hx @ w_in, IN_SIZES)
    bsz = hx.shape[0]

    u_c = dwconv1d(lx_c, lru_conv_w, lru_conv_b).astype(jnp.float32)
    u_x = dwconv1d(lx_x, lru_conv_w, lru_conv_b).astype(jnp.float32)
    h0 = jnp.zeros((bsz, LRU_WIDTH), jnp.float32)
    ya_c, ya_x = [], []
    for d in range(N_DIR):
        hs_c = rglru(rev(u_c, d), lru_wa[d], lru_ba[d], lru_wx[d], lru_bx[d], lru_lambda[d], h0)
        hs_x = rglru(rev(u_x, d), lru_wa[d], lru_ba[d], lru_wx[d], lru_bx[d], lru_lambda[d], hs_c[:, -1])
        if ctx_out:
            ya_c.append(rev(hs_c, d))
        ya_x.append(rev(hs_x, d))

    xs_c, b_c, c_c = ssd_heads(jax.nn.silu(dwconv1d(xbc_c, ssd_conv_w, ssd_conv_b)))
    xs_x, b_x, c_x = ssd_heads(jax.nn.silu(dwconv1d(xbc_x, ssd_conv_w, ssd_conv_b)))
    s0 = jnp.zeros((bsz, SSD_HEADS, SSD_HEAD_DIM, SSD_STATE), jnp.float32)
    yb_c, yb_x = [], []
    for d in range(N_DIR):
        a = -jnp.exp(ssd_a_log[d].astype(jnp.float32))
        bias = ssd_dt_bias[d].astype(jnp.float32)
        dtd_c = jax.nn.softplus(dt_c[..., d * SSD_HEADS:(d + 1) * SSD_HEADS].astype(jnp.float32) + bias)
        dtd_x = jax.nn.softplus(dt_x[..., d * SSD_HEADS:(d + 1) * SSD_HEADS].astype(jnp.float32) + bias)
        if ctx_out:
            y_c, st_c = ssd_chunked(rev(xs_c, d), rev(dtd_c, d), a, rev(b_c, d), rev(c_c, d), s0)
            yb_c.append(rev(y_c, d))
        else:
            st_c = ssd_final_state(rev(xs_c, d), rev(dtd_c, d), a, rev(b_c, d))
        y_x, _ = ssd_chunked(rev(xs_x, d), rev(dtd_x, d), a, rev(b_x, d), rev(c_x, d), st_c)
        yb_x.append(rev(y_x, d))

    def merge(ya, lg, yb, xs, z, gt):
        ra = ((ya[0] + ya[1]).astype(lg.dtype) * jax.nn.gelu(lg)) @ lru_proj
        rb = ssd_output(yb[0] + yb[1], xs, z, ssd_d, ssd_norm_w) @ ssd_proj
        ga, gb = jnp.split(jax.nn.sigmoid(gt), 2, axis=-1)
        return (ga * ra + gb * rb) @ w_out

    out_x = merge(ya_x, lg_x, yb_x, xs_x, z_x, gt_x)
    out_c = merge(ya_c, lg_c, yb_c, xs_c, z_c, gt_c) if ctx_out else None
    return out_c, out_x


def conv_ffn(h, w_up, conv_w, conv_b, w_down, rows, cols):
    u, v = jnp.split(h @ w_up, 2, axis=-1)
    u = dwconv2d(u, conv_w, conv_b, rows, cols)
    return (jax.nn.gelu(u) * v) @ w_down


def setup_inputs(seed: int = 0) -> dict:
    key = jax.random.key(seed)
    ks = jax.random.split(key, 32)

    def nrm(k, shape, scale):
        return jax.random.normal(k, shape, jnp.float32) * scale

    L = DEPTH
    a0 = jax.random.uniform(ks[14], (L, N_DIR, LRU_WIDTH), jnp.float32, 0.9, 0.999)
    s = a0 ** (1.0 / LRU_C)
    lam = jnp.log(s) - jnp.log1p(-s)
    dt0 = jnp.exp(jax.random.uniform(ks[18], (L, N_DIR, SSD_HEADS), jnp.float32, math.log(1e-3), math.log(1e-1)))
    dt_bias = dt0 + jnp.log(-jnp.expm1(-dt0))
    a_log = jnp.log(jax.random.uniform(ks[19], (L, N_DIR, SSD_HEADS), jnp.float32, 1.0, 16.0))
    return {
        "x": nrm(ks[0], (BATCH, SEQ, D_MODEL), 1.0),
        "c": nrm(ks[1], (BATCH, D_MODEL), 1.0),
        "ctx": nrm(ks[2], (BATCH, CTX_LEN, D_MODEL), 1.0),
        "c_ctx": nrm(ks[3], (D_MODEL,), 1.0),
        "ada_w": nrm(ks[4], (L, D_MODEL, 6 * D_MODEL), 0.5 * D_MODEL ** -0.5),
        "ada_b": nrm(ks[5], (L, 6 * D_MODEL), 0.02),
        "norm_mix_w": 1.0 + nrm(ks[6], (L, D_MODEL), 0.05),
        "norm_ffn_w": 1.0 + nrm(ks[7], (L, D_MODEL), 0.05),
        "w_in": nrm(ks[8], (L, D_MODEL, IN_DIM), D_MODEL ** -0.5),
        "lru_conv_w": nrm(ks[9], (L, LRU_CONV, LRU_WIDTH), LRU_CONV ** -0.5),
        "lru_conv_b": nrm(ks[10], (L, LRU_WIDTH), 0.02),
        "lru_wa": nrm(ks[11], (L, N_DIR, LRU_HEADS, LRU_HEAD_DIM, LRU_HEAD_DIM), LRU_HEAD_DIM ** -0.5),
        "lru_ba": nrm(ks[12], (L, N_DIR, LRU_WIDTH), 0.02),
        "lru_wx": nrm(ks[13], (L, N_DIR, LRU_HEADS, LRU_HEAD_DIM, LRU_HEAD_DIM), LRU_HEAD_DIM ** -0.5),
        "lru_bx": nrm(ks[15], (L, N_DIR, LRU_WIDTH), 0.02),
        "lru_lambda": lam,
        "lru_proj": nrm(ks[16], (L, LRU_WIDTH, D_MODEL), LRU_WIDTH ** -0.5),
        "ssd_conv_w": nrm(ks[17], (L, SSD_CONV, SSD_CONV_DIM), SSD_CONV ** -0.5),
        "ssd_conv_b": nrm(ks[20], (L, SSD_CONV_DIM), 0.02),
        "ssd_dt_bias": dt_bias,
        "ssd_a_log": a_log,
        "ssd_d": 1.0 + nrm(ks[21], (L, SSD_HEADS), 0.1),
        "ssd_norm_w": 1.0 + nrm(ks[22], (L, SSD_INNER), 0.05),
        "ssd_proj": nrm(ks[23], (L, SSD_INNER, D_MODEL), SSD_INNER ** -0.5),
        "w_out": nrm(ks[24], (L, D_MODEL, D_MODEL), D_MODEL ** -0.5),
        "ffn_w_up": nrm(ks[25], (L, D_MODEL, 2 * FFN_DIM), D_MODEL ** -0.5),
        "ffn_conv_w": nrm(ks[26], (L, FFN_CONV, FFN_CONV, FFN_DIM), 1.0 / FFN_CONV),
        "ffn_conv_b": nrm(ks[27], (L, FFN_DIM), 0.02),
        "ffn_w_down": nrm(ks[28], (L, FFN_DIM, D_MODEL), FFN_DIM ** -0.5),
        "final_norm_w": 1.0 + nrm(ks[29], (D_MODEL,), 0.05),
    }


def reference(x, c, ctx, c_ctx, ada_w, ada_b, norm_mix_w, norm_ffn_w, w_in, lru_conv_w, lru_conv_b,
              lru_wa, lru_ba, lru_wx, lru_bx, lru_lambda, lru_proj, ssd_conv_w, ssd_conv_b, ssd_dt_bias,
              ssd_a_log, ssd_d, ssd_norm_w, ssd_proj, w_out, ffn_w_up, ffn_conv_w, ffn_conv_b, ffn_w_down,
              final_norm_w):
    rows = x.shape[1] // GRID_W
    ctx_len = ctx.shape[1]
    sc = jax.nn.silu(c)
    scc = jax.nn.silu(c_ctx)
    for l in range(DEPTH):
        last = l == DEPTH - 1
        mx = jnp.split((sc @ ada_w[l] + ada_b[l])[:, None, :], 6, axis=-1)
        mc = jnp.split((scc @ ada_w[l] + ada_b[l])[None, None, :], 6, axis=-1)
        hx = modulate(rmsnorm(x, norm_mix_w[l]), mx[0], mx[1])
        hc = modulate(rmsnorm(ctx, norm_mix_w[l]), mc[0], mc[1])
        oc, ox = mixer(hc, hx, w_in[l], lru_conv_w[l], lru_conv_b[l], lru_wa[l], lru_ba[l], lru_wx[l],
                       lru_bx[l], lru_lambda[l], lru_proj[l], ssd_conv_w[l], ssd_conv_b[l], ssd_dt_bias[l],
                       ssd_a_log[l], ssd_d[l], ssd_norm_w[l], ssd_proj[l], w_out[l], not last)
        x = x + mx[2] * ox
        hx2 = modulate(rmsnorm(x, norm_ffn_w[l]), mx[3], mx[4])
        x = x + mx[5] * conv_ffn(hx2, ffn_w_up[l], ffn_conv_w[l], ffn_conv_b[l], ffn_w_down[l], rows, GRID_W)
        if not last:
            ctx = ctx + mc[2] * oc
            hc2 = modulate(rmsnorm(ctx, norm_ffn_w[l]), mc[3], mc[4])
            ctx = ctx + mc[5] * conv_ffn(hc2, ffn_w_up[l], ffn_conv_w[l], ffn_conv_b[l], ffn_w_down[l], 1, ctx_len)
    return rmsnorm(x, final_norm_w)
```

```python
import functools

import jax
import jax.numpy as jnp
from jax import lax
from jax.experimental import pallas as pl
from jax.experimental.pallas import tpu as pltpu

F32 = jnp.float32
BF16 = jnp.bfloat16

EPS = 1e-6
D_MODEL = 1024
LRU_WIDTH = 1024
LRU_HEADS = 16
LRU_HEAD_DIM = 64
LRU_C = 8.0
SSD_INNER = 2048
SSD_HEADS = 32
SSD_HEAD_DIM = 64
SSD_GROUPS = 4
SSD_STATE = 128
SSD_CONV_DIM = SSD_INNER + 2 * SSD_GROUPS * SSD_STATE
SSD_GROUP_W = SSD_INNER // SSD_GROUPS
FFN_DIM = 2816

LANES = 128
SUBLANES = 8
VMEM_LIMIT_BYTES = 56 * 1024 * 1024

OFF_Z = 0
OFF_GT = OFF_Z + SSD_INNER
OFF_LX = OFF_GT + 2 * D_MODEL
OFF_LG = OFF_LX + LRU_WIDTH
OFF_XBC = OFF_LG + LRU_WIDTH
BIG_W = OFF_XBC + SSD_CONV_DIM
DT_W = 128

CBLK = 256
Q = 128
FCH = 256


def _cparams(sem):
    return pltpu.CompilerParams(dimension_semantics=sem, vmem_limit_bytes=VMEM_LIMIT_BYTES)


def _gelu_tanh(x):
    return 0.5 * x * (1.0 + jnp.tanh(0.7978845608028654 * (x + 0.044715 * (x * x * x))))


def _sigmoid(x):
    return 1.0 / (1.0 + jnp.exp(-x))


def _norm_mod(x, nw, shift, scale):
    ms = jnp.mean(x * x, axis=-1, keepdims=True)
    y = x * lax.rsqrt(ms + EPS) * nw
    return y * (1.0 + scale) + shift


def _pair_expand(v, j, lane_lo, off=0):
    return jnp.where(lane_lo, v[:, off + 2 * j:off + 2 * j + 1], v[:, off + 2 * j + 1:off + 2 * j + 2])


def _tri_dot(tri_bf16, v):
    hi = v.astype(BF16)
    r1 = v - hi.astype(F32)
    mid = r1.astype(BF16)
    lo = (r1 - mid.astype(F32)).astype(BF16)
    return (jnp.dot(tri_bf16, hi, preferred_element_type=F32)
            + jnp.dot(tri_bf16, mid, preferred_element_type=F32)
            + jnp.dot(tri_bf16, lo, preferred_element_type=F32))


def _ada_kernel(s_ref, w_ref, b_ref, o_ref):
    s = s_ref[...]
    s = s * _sigmoid(s)
    o_ref[...] = jnp.dot(s, w_ref[...], preferred_element_type=F32,
                         precision=lax.Precision.HIGHEST) + b_ref[...]


def _ada_call(s, ada_w, ada_b):
    depth, d, n = ada_w.shape
    tn = 1536
    rows = s.shape[0]
    return pl.pallas_call(
        _ada_kernel,
        grid=(depth, n // tn),
        in_specs=[pl.BlockSpec((rows, d), lambda l, j: (0, 0)),
                  pl.BlockSpec((None, d, tn), lambda l, j: (l, 0, j)),
                  pl.BlockSpec((None, 1, tn), lambda l, j: (l, 0, j))],
        out_specs=pl.BlockSpec((None, rows, tn), lambda l, j: (l, 0, j)),
        out_shape=jax.ShapeDtypeStruct((depth, rows, n), F32),
        compiler_params=_cparams(("arbitrary", "arbitrary")),
        name="ada",
    )(s, ada_w, ada_b.reshape(depth, 1, n))


def _inproj_kernel(x_ref, nw_ref, mod_ref, w_ref, wdt_ref, big_ref, dt_ref, h_scr):
    @pl.when(pl.program_id(2) == 0)
    def _():
        h = _norm_mod(x_ref[...], nw_ref[...], mod_ref[0:1, :], mod_ref[1:2, :]).astype(BF16)
        h_scr[...] = h
        dt_ref[...] = jnp.dot(h, wdt_ref[...], preferred_element_type=F32)

    big_ref[...] = jnp.dot(h_scr[...], w_ref[...], preferred_element_type=F32)


def _inproj_call(x, nw, mod, w, wdt):
    b, l, d = x.shape
    tm = min(1024, l)
    tn = 1024
    return pl.pallas_call(
        _inproj_kernel,
        grid=(b, l // tm, BIG_W // tn),
        in_specs=[pl.BlockSpec((None, tm, d), lambda bi, i, j: (bi, i, 0)),
                  pl.BlockSpec((1, d), lambda bi, i, j: (0, 0)),
                  pl.BlockSpec((None, 6, d), lambda bi, i, j: (bi, 0, 0)),
                  pl.BlockSpec((d, tn), lambda bi, i, j: (0, j)),
                  pl.BlockSpec((d, DT_W), lambda bi, i, j: (0, 0))],
        out_specs=[pl.BlockSpec((None, tm, tn), lambda bi, i, j: (bi, i, j)),
                   pl.BlockSpec((None, tm, DT_W), lambda bi, i, j: (bi, i, 0))],
        out_shape=[jax.ShapeDtypeStruct((b, l, BIG_W), F32),
                   jax.ShapeDtypeStruct((b, l, DT_W), F32)],
        scratch_shapes=[pltpu.VMEM((tm, d), BF16)],
        compiler_params=_cparams(("arbitrary", "arbitrary", "arbitrary")),
        name="in_proj",
    )(x, nw, mod, w, wdt)


def _conv4_tile(x_ref, cw_ref, cb_ref, i, n_tiles, t, l):
    r0 = pl.multiple_of(i * t, t)
    main = x_ref[pl.ds(r0, t), :]
    prev8 = x_ref[pl.ds(pl.multiple_of(jnp.maximum(r0 - SUBLANES, 0), SUBLANES), SUBLANES), :]
    next8 = x_ref[pl.ds(pl.multiple_of(jnp.minimum(r0 + t, l - SUBLANES), SUBLANES), SUBLANES), :]
    prev8 = jnp.where(i > 0, prev8, 0.0)
    next8 = jnp.where(i < n_tiles - 1, next8, 0.0)
    win = jnp.concatenate([prev8, main, next8], axis=0)
    acc = cb_ref[...] + cw_ref[0:1, :] * win[6:6 + t]
    for k in range(1, 4):
        acc = acc + cw_ref[k:k + 1, :] * win[6 + k:6 + k + t]
    return acc


def _lru_kernel(lx_ref, lg_ref, cw_ref, cb_ref, wg_ref, bg_ref, lam_ref, h0_ref,
                o_ref, hfin_ref, u_scr, y_scr, a_scr, b_scr, *, l, t):
    n_tiles = l // t
    n_grp = t // SUBLANES
    c = CBLK
    lam = lam_ref[...]
    c8 = LRU_C * (jnp.minimum(lam, 0.0) - jnp.log1p(jnp.exp(-jnp.abs(lam))))
    row8 = lax.broadcasted_iota(jnp.int32, (SUBLANES, c), 0)

    def gates(u, d):
        g = jnp.dot(u.astype(BF16), wg_ref[:, d * 2 * c:(d + 1) * 2 * c],
                    preferred_element_type=F32) + bg_ref[:, d * 2 * c:(d + 1) * 2 * c]
        r = _sigmoid(g[:, :c])
        ig = _sigmoid(g[:, c:])
        log_a = c8[d:d + 1, :] * r
        a = jnp.exp(log_a)
        bb = jnp.sqrt(-jnp.tanh(log_a) * (a * a + 1.0)) * (ig * u)
        a_scr[...] = a
        b_scr[...] = bb

    def fwd_tile(i, hprev):
        r0 = pl.multiple_of(i * t, t)
        u = _conv4_tile(lx_ref, cw_ref, cb_ref, i, n_tiles, t, l)
        u_scr[pl.ds(r0, t), :] = u
        gates(u, 0)

        def grp(g, hp):
            r = pl.multiple_of(g * SUBLANES, SUBLANES)
            av = a_scr[pl.ds(r, SUBLANES), :]
            bv = b_scr[pl.ds(r, SUBLANES), :]
            for k in (1, 2, 4):
                a_s = pltpu.roll(av, k, 0)
                b_s = pltpu.roll(bv, k, 0)
                m = row8 >= k
                bv = jnp.where(m, av * b_s + bv, bv)
                av = jnp.where(m, av * a_s, av)
            h = bv + av * hp
            y_scr[pl.ds(r0 + r, SUBLANES), :] = h
            return h[SUBLANES - 1:SUBLANES, :]

        return lax.fori_loop(0, n_grp, grp, hprev)

    hf = lax.fori_loop(0, n_tiles, fwd_tile, h0_ref[0:1, :])
    hfin_ref[0:1, :] = hf

    def bwd_tile(ii, hnext):
        i = n_tiles - 1 - ii
        r0 = pl.multiple_of(i * t, t)
        u = u_scr[pl.ds(r0, t), :]
        gates(u, 1)

        def grp(gg, hn):
            g = n_grp - 1 - gg
            r = pl.multiple_of(g * SUBLANES, SUBLANES)
            av = a_scr[pl.ds(r, SUBLANES), :]
            bv = b_scr[pl.ds(r, SUBLANES), :]
            for k in (1, 2, 4):
                a_s = pltpu.roll(av, SUBLANES - k, 0)
                b_s = pltpu.roll(bv, SUBLANES - k, 0)
                m = row8 < SUBLANES - k
                bv = jnp.where(m, av * b_s + bv, bv)
                av = jnp.where(m, av * a_s, av)
            h = bv + av * hn
            y_scr[pl.ds(r0 + r, SUBLANES), :] = y_scr[pl.ds(r0 + r, SUBLANES), :] + h
            return h[0:1, :]

        hn = lax.fori_loop(0, n_grp, grp, hnext)
        o_ref[pl.ds(r0, t), :] = (y_scr[pl.ds(r0, t), :] * _gelu_tanh(lg_ref[pl.ds(r0, t), :])).astype(BF16)
        return hn

    hb = lax.fori_loop(0, n_tiles, bwd_tile, h0_ref[1:2, :])
    hfin_ref[1:2, :] = hb


def _lru_call(big, cw, cb, wg, bg, lam, h0):
    b, l, _ = big.shape
    t = 256
    nblk = LRU_WIDTH // CBLK
    lx0 = OFF_LX // CBLK
    lg0 = OFF_LG // CBLK
    kern = functools.partial(_lru_kernel, l=l, t=t)
    return pl.pallas_call(
        kern,
        grid=(b, nblk),
        in_specs=[pl.BlockSpec((None, l, CBLK), lambda bi, ci: (bi, 0, lx0 + ci)),
                  pl.BlockSpec((None, l, CBLK), lambda bi, ci: (bi, 0, lg0 + ci)),
                  pl.BlockSpec((4, CBLK), lambda bi, ci: (0, ci)),
                  pl.BlockSpec((1, CBLK), lambda bi, ci: (0, ci)),
                  pl.BlockSpec((None, CBLK, 4 * CBLK), lambda bi, ci: (ci, 0, 0)),
                  pl.BlockSpec((None, 1, 4 * CBLK), lambda bi, ci: (ci, 0, 0)),
                  pl.BlockSpec((2, CBLK), lambda bi, ci: (0, ci)),
                  pl.BlockSpec((None, 2, CBLK), lambda bi, ci: (bi, 0, ci))],
        out_specs=[pl.BlockSpec((None, l, CBLK), lambda bi, ci: (bi, 0, ci)),
                   pl.BlockSpec((None, 2, CBLK), lambda bi, ci: (bi, 0, ci))],
        out_shape=[jax.ShapeDtypeStruct((b, l, LRU_WIDTH), BF16),
                   jax.ShapeDtypeStruct((b, 2, LRU_WIDTH), F32)],
        scratch_shapes=[pltpu.VMEM((l, CBLK), F32), pltpu.VMEM((l, CBLK), F32),
                        pltpu.VMEM((t, CBLK), F32), pltpu.VMEM((t, CBLK), F32)],
        compiler_params=_cparams(("arbitrary", "arbitrary")),
        name="lru",
    )(big, big, cw, cb, wg, bg, lam, h0)


def _ssdprep_kernel(x_ref, cw_ref, cb_ref, dt_ref, dtb_ref, o_ref, dto_ref, *, l, t):
    n_tiles = l // t

    def tile(i, carry):
        r0 = pl.multiple_of(i * t, t)
        y = _conv4_tile(x_ref, cw_ref, cb_ref, i, n_tiles, t, l)
        o_ref[pl.ds(r0, t), :] = (y * _sigmoid(y)).astype(BF16)
        return carry

    lax.fori_loop(0, n_tiles, tile, 0)

    @pl.when(pl.program_id(1) == 0)
    def _():
        v = dt_ref[...] + dtb_ref[...]
        dto_ref[...] = jnp.maximum(v, 0.0) + jnp.log1p(jnp.exp(-jnp.abs(v)))


def _ssdprep_call(big, dt, cw, cb, dtb):
    b, l, _ = big.shape
    t = 256
    nblk = SSD_CONV_DIM // CBLK
    x0 = OFF_XBC // CBLK
    kern = functools.partial(_ssdprep_kernel, l=l, t=t)
    return pl.pallas_call(
        kern,
        grid=(b, nblk),
        in_specs=[pl.BlockSpec((None, l, CBLK), lambda bi, ci: (bi, 0, x0 + ci)),
                  pl.BlockSpec((4, CBLK), lambda bi, ci: (0, ci)),
                  pl.BlockSpec((1, CBLK), lambda bi, ci: (0, ci)),
                  pl.BlockSpec((None, l, DT_W), lambda bi, ci: (bi, 0, 0)),
                  pl.BlockSpec((1, DT_W), lambda bi, ci: (0, 0))],
        out_specs=[pl.BlockSpec((None, l, CBLK), lambda bi, ci: (bi, 0, ci)),
                   pl.BlockSpec((None, l, DT_W), lambda bi, ci: (bi, 0, 0))],
        out_shape=[jax.ShapeDtypeStruct((b, l, SSD_CONV_DIM), BF16),
                   jax.ShapeDtypeStruct((b, l, DT_W), F32)],
        compiler_params=_cparams(("arbitrary", "arbitrary")),
        name="ssd_prep",
    )(big, cw, cb, dt, dtb)


def _ssdstate_kernel(xf_ref, xb_ref, dtf_ref, dtb_ref, a_ref, s0_ref,
                     sef_ref, seb_ref, sfin_ref, sf_scr, sb_scr, xw_scr):
    c = pl.program_id(1)
    nc = pl.num_programs(1)

    @pl.when(c == 0)
    def _():
        sf_scr[...] = s0_ref[0]
        sb_scr[...] = s0_ref[1]

    a = a_ref[...]
    row = lax.broadcasted_iota(jnp.int32, (Q, Q), 0)
    col = lax.broadcasted_iota(jnp.int32, (Q, Q), 1)
    lane_lo = lax.broadcasted_iota(jnp.int32, (1, LANES), 1) < SSD_HEAD_DIM
    hp = SSD_HEADS // 2
    gp = hp // SSD_GROUPS

    for d in range(2):
        x_ref = (xf_ref, xb_ref)[d]
        dt = (dtf_ref, dtb_ref)[d][...]
        off = d * SSD_HEADS
        s_scr = (sf_scr, sb_scr)[d]
        se_ref = (sef_ref, seb_ref)[d]
        tri = jnp.where(col <= row, 1.0, 0.0) if d == 0 else jnp.where(col >= row, 1.0, 0.0)
        cs = _tri_dot(tri.astype(BF16), dt * a)
        tot = cs[Q - 1:Q, :] if d == 0 else cs[0:1, :]
        w = jnp.exp(tot - cs) * dt
        dec = jnp.exp(tot)
        se_ref[...] = s_scr[...].astype(BF16)
        for j in range(hp):
            sl = slice(j * LANES, (j + 1) * LANES)
            xw_scr[:, sl] = (x_ref[:, sl].astype(F32) * _pair_expand(w, j, lane_lo, off)).astype(BF16)
        for g in range(SSD_GROUPS):
            bg = x_ref[:, SSD_INNER + g * SSD_STATE:SSD_INNER + (g + 1) * SSD_STATE]
            sloc = lax.dot_general(bg, xw_scr[:, g * SSD_GROUP_W:(g + 1) * SSD_GROUP_W],
                                   (((0,), (0,)), ((), ())), preferred_element_type=F32)
            for jj in range(gp):
                j = g * gp + jj
                sl = slice(j * LANES, (j + 1) * LANES)
                s_scr[:, sl] = (s_scr[:, sl] * _pair_expand(dec, j, lane_lo, off)
                                + sloc[:, jj * LANES:(jj + 1) * LANES])

    @pl.when(c == nc - 1)
    def _():
        sfin_ref[0] = sf_scr[...]
        sfin_ref[1] = sb_scr[...]


def _ssdstate_call(xact, dtsp, a128, s0):
    b, l, _ = xact.shape
    nc = l // Q
    return pl.pallas_call(
        _ssdstate_kernel,
        grid=(b, nc),
        in_specs=[pl.BlockSpec((None, Q, SSD_CONV_DIM), lambda bi, ci: (bi, ci, 0)),
                  pl.BlockSpec((None, Q, SSD_CONV_DIM), lambda bi, ci: (bi, nc - 1 - ci, 0)),
                  pl.BlockSpec((None, Q, DT_W), lambda bi, ci: (bi, ci, 0)),
                  pl.BlockSpec((None, Q, DT_W), lambda bi, ci: (bi, nc - 1 - ci, 0)),
                  pl.BlockSpec((1, DT_W), lambda bi, ci: (0, 0)),
                  pl.BlockSpec((None, 2, SSD_STATE, SSD_INNER), lambda bi, ci: (bi, 0, 0, 0))],
        out_specs=[pl.BlockSpec((None, None, SSD_STATE, SSD_INNER), lambda bi, ci: (bi, ci, 0, 0)),
                   pl.BlockSpec((None, None, SSD_STATE, SSD_INNER), lambda bi, ci: (bi, nc - 1 - ci, 0, 0)),
                   pl.BlockSpec((None, 2, SSD_STATE, SSD_INNER), lambda bi, ci: (bi, 0, 0, 0))],
        out_shape=[jax.ShapeDtypeStruct((b, nc, SSD_STATE, SSD_INNER), BF16),
                   jax.ShapeDtypeStruct((b, nc, SSD_STATE, SSD_INNER), BF16),
                   jax.ShapeDtypeStruct((b, 2, SSD_STATE, SSD_INNER), F32)],
        scratch_shapes=[pltpu.VMEM((SSD_STATE, SSD_INNER), F32), pltpu.VMEM((SSD_STATE, SSD_INNER), F32),
                        pltpu.VMEM((Q, SSD_INNER), BF16)],
        compiler_params=_cparams(("arbitrary", "arbitrary")),
        name="ssd_states",
    )(xact, xact, dtsp, dtsp, a128, s0)


def _ssdy_kernel(x_ref, dt_ref, a_ref, dsk_ref, sef_ref, seb_ref, z_ref, nw_ref, o_ref, y_scr):
    nh = SSD_HEADS
    row = lax.broadcasted_iota(jnp.int32, (Q, Q), 0)
    col = lax.broadcasted_iota(jnp.int32, (Q, Q), 1)
    low = col <= row
    lane = lax.broadcasted_iota(jnp.int32, (1, LANES), 1)
    lane_lo = lane < SSD_HEAD_DIM
    dt = dt_ref[...]
    da = dt * a_ref[...]
    cs_f = _tri_dot(jnp.where(low, 1.0, 0.0).astype(BF16), da)
    cs_b = _tri_dot(jnp.where(col >= row, 1.0, 0.0).astype(BF16), da)
    p = jnp.where(lane < nh, cs_f, jnp.where(lane < 2 * nh, cs_b, dt))
    pt = p.T
    sc = jnp.exp(p)
    hp_per_g = nh // 2 // SSD_GROUPS

    for g in range(SSD_GROUPS):
        bg = x_ref[:, SSD_INNER + g * SSD_STATE:SSD_INNER + (g + 1) * SSD_STATE]
        cg = x_ref[:, SSD_INNER + (SSD_GROUPS + g) * SSD_STATE:SSD_INNER + (SSD_GROUPS + g + 1) * SSD_STATE]
        cb = lax.dot_general(cg, bg, (((1,), (1,)), ((), ())), preferred_element_type=F32)
        dcb = jnp.sum(cg.astype(F32) * bg.astype(F32), axis=1, keepdims=True)
        gsl = slice(g * SSD_GROUP_W, (g + 1) * SSD_GROUP_W)
        yoff_f = jnp.dot(cg, sef_ref[:, gsl], preferred_element_type=F32)
        yoff_b = jnp.dot(cg, seb_ref[:, gsl], preferred_element_type=F32)
        dcorr = dcb * p + dsk_ref[...]
        for jj in range(hp_per_g):
            j = g * hp_per_g + jj
            sl = slice(j * LANES, (j + 1) * LANES)
            xp = x_ref[:, sl]
            res = []
            for h in (2 * j, 2 * j + 1):
                arg = jnp.where(low, p[:, h:h + 1] - pt[h:h + 1, :],
                                p[:, nh + h:nh + h + 1] - pt[nh + h:nh + h + 1, :])
                dts = jnp.where(low, pt[2 * nh + h:2 * nh + h + 1, :], pt[3 * nh + h:3 * nh + h + 1, :])
                m = (cb * jnp.exp(arg) * dts).astype(BF16)
                res.append(jnp.dot(m, xp, preferred_element_type=F32))
            y = (jnp.where(lane_lo, res[0], res[1])
                 + _pair_expand(sc, j, lane_lo) * yoff_f[:, jj * LANES:(jj + 1) * LANES]
                 + _pair_expand(sc, j, lane_lo, nh) * yoff_b[:, jj * LANES:(jj + 1) * LANES]
                 + _pair_expand(dcorr, j, lane_lo, 3 * nh) * xp.astype(F32))
            y_scr[:, sl] = y
        yg = y_scr[:, gsl]
        zg = z_ref[:, gsl]
        gated = yg * (zg * _sigmoid(zg))
        ms = jnp.mean(gated * gated, axis=-1, keepdims=True)
        o_ref[:, gsl] = (gated * lax.rsqrt(ms + EPS) * nw_ref[:, gsl]).astype(BF16)


def _ssdy_call(xact, dtsp, a128, dskip, sef, seb, big, nw):
    b, l, _ = xact.shape
    nc = l // Q
    return pl.pallas_call(
        _ssdy_kernel,
        grid=(b, nc),
        in_specs=[pl.BlockSpec((None, Q, SSD_CONV_DIM), lambda bi, ci: (bi, ci, 0)),
                  pl.BlockSpec((None, Q, DT_W), lambda bi, ci: (bi, ci, 0)),
                  pl.BlockSpec((1, DT_W), lambda bi, ci: (0, 0)),
                  pl.BlockSpec((1, DT_W), lambda bi, ci: (0, 0)),
                  pl.BlockSpec((None, None, SSD_STATE, SSD_INNER), lambda bi, ci: (bi, ci, 0, 0)),
                  pl.BlockSpec((None, None, SSD_STATE, SSD_INNER), lambda bi, ci: (bi, ci, 0, 0)),
                  pl.BlockSpec((None, Q, SSD_INNER), lambda bi, ci: (bi, ci, OFF_Z // SSD_INNER)),
                  pl.BlockSpec((1, SSD_INNER), lambda bi, ci: (0, 0))],
        out_specs=pl.BlockSpec((None, Q, SSD_INNER), lambda bi, ci: (bi, ci, 0)),
        out_shape=jax.ShapeDtypeStruct((b, l, SSD_INNER), BF16),
        scratch_shapes=[pltpu.VMEM((Q, SSD_INNER), F32)],
        compiler_params=_cparams(("arbitrary", "arbitrary")),
        name="ssd_y",
    )(xact, dtsp, a128, dskip, sef, seb, big, nw)


def _merge_kernel(ra_ref, gn_ref, gt_ref, x_ref, mod_ref, wl_ref, ws_ref, wo_ref, o_ref):
    ra = jnp.dot(ra_ref[...], wl_ref[...], preferred_element_type=F32)
    rb = jnp.dot(gn_ref[...], ws_ref[...], preferred_element_type=F32)
    ga = _sigmoid(gt_ref[:, :D_MODEL])
    gb = _sigmoid(gt_ref[:, D_MODEL:])
    m = (ga * ra + gb * rb).astype(BF16)
    o = jnp.dot(m, wo_ref[...], preferred_element_type=F32)
    o_ref[...] = x_ref[...] + mod_ref[2:3, :] * o


def _merge_call(ra, gn, big, x, mod, wl, ws, wo):
    b, l, d = x.shape
    tm = min(512, l)
    const = lambda bi, i: (0, 0)
    return pl.pallas_call(
        _merge_kernel,
        grid=(b, l // tm),
        in_specs=[pl.BlockSpec((None, tm, LRU_WIDTH), lambda bi, i: (bi, i, 0)),
                  pl.BlockSpec((None, tm, SSD_INNER), lambda bi, i: (bi, i, 0)),
                  pl.BlockSpec((None, tm, 2 * d), lambda bi, i: (bi, i, OFF_GT // (2 * D_MODEL))),
                  pl.BlockSpec((None, tm, d), lambda bi, i: (bi, i, 0)),
                  pl.BlockSpec((None, 6, d), lambda bi, i: (bi, 0, 0)),
                  pl.BlockSpec((LRU_WIDTH, d), const),
                  pl.BlockSpec((SSD_INNER, d), const),
                  pl.BlockSpec((d, d), const)],
        out_specs=pl.BlockSpec((None, tm, d), lambda bi, i: (bi, i, 0)),
        out_shape=jax.ShapeDtypeStruct((b, l, d), F32),
        compiler_params=_cparams(("arbitrary", "arbitrary")),
        name="merge",
    )(ra, gn, big, x, mod, wl, ws, wo)


def _ffn_kernel(*refs, tm, gw, halo, final_norm):
    if halo:
        (x_ref, xp_ref, xn_ref, nw_ref, mod_ref, wu_ref, cw_ref, cb_ref, wd_ref, fnw_ref,
         o_ref, h_scr, u_scr, acc_scr) = refs
    else:
        (x_ref, nw_ref, mod_ref, wu_ref, cw_ref, cb_ref, wd_ref, fnw_ref,
         o_ref, h_scr, u_scr, acc_scr) = refs
    i = pl.program_id(1)
    n_i = pl.num_programs(1)
    nw = nw_ref[...]
    shift = mod_ref[3:4, :]
    scale = mod_ref[4:5, :]
    ext = tm + 2 * gw if halo else tm
    top = gw if halo else 0
    pad = SUBLANES

    x = x_ref[...]
    h_scr[top:top + tm, :] = _norm_mod(x, nw, shift, scale).astype(BF16)
    if halo:
        hp = _norm_mod(xp_ref[...], nw, shift, scale)
        hn = _norm_mod(xn_ref[...], nw, shift, scale)
        h_scr[0:gw, :] = jnp.where(i > 0, hp, 0.0).astype(BF16)
        h_scr[gw + tm:ext, :] = jnp.where(i < n_i - 1, hn, 0.0).astype(BF16)
    u_scr[0:pad, :] = jnp.zeros((pad, FCH), F32)
    u_scr[pad + ext:pad + ext + pad, :] = jnp.zeros((pad, FCH), F32)
    acc_scr[...] = jnp.zeros_like(acc_scr)

    def chunk(ci, carry):
        c0 = pl.multiple_of(ci * FCH, FCH)
        colid = lax.broadcasted_iota(jnp.int32, (ext, FCH), 0) & (gw - 1)
        has_left = colid >= 1
        has_right = colid <= gw - 2
        u = jnp.dot(h_scr[...], wu_ref[:, pl.ds(c0, FCH)], preferred_element_type=F32)
        v = jnp.dot(h_scr[top:top + tm, :], wu_ref[:, pl.ds(FFN_DIM + c0, FCH)],
                    preferred_element_type=F32)
        u_scr[pad:pad + ext, :] = u
        ul = jnp.where(has_left, u_scr[pad - 1:pad - 1 + ext, :], 0.0)
        ur = jnp.where(has_right, u_scr[pad + 1:pad + 1 + ext, :], 0.0)
        cw = cw_ref[:, pl.ds(c0, FCH)]
        acc = cb_ref[:, pl.ds(c0, FCH)]
        for dr in ((-1, 0, 1) if halo else (0,)):
            o = top + dr * gw
            k = (dr + 1) * 3
            acc = (acc + cw[k:k + 1, :] * ul[o:o + tm] + cw[k + 1:k + 2, :] * u[o:o + tm]
                   + cw[k + 2:k + 3, :] * ur[o:o + tm])
        act = (_gelu_tanh(acc) * v).astype(BF16)
        acc_scr[...] += jnp.dot(act, wd_ref[pl.ds(c0, FCH), :], preferred_element_type=F32)
        return carry

    lax.fori_loop(0, FFN_DIM // FCH, chunk, 0)
    y = x + mod_ref[5:6, :] * acc_scr[...]
    if final_norm:
        ms = jnp.mean(y * y, axis=-1, keepdims=True)
        y = y * lax.rsqrt(ms + EPS) * fnw_ref[...]
    o_ref[...] = y


def _ffn_call(x, nw, mod, wu, cw, cb, wd, fnw, *, gw, final_norm):
    b, l, d = x.shape
    rows = l // gw
    halo = rows > 1
    tm = min(512, l)
    assert tm % gw == 0 and gw & (gw - 1) == 0
    ext = tm + 2 * gw if halo else tm
    r = tm // gw
    ng = l // gw
    const = lambda bi, i: (0, 0)
    single = pl.Buffered(1)
    in_specs = [pl.BlockSpec((None, tm, d), lambda bi, i: (bi, i, 0))]
    args = [x]
    if halo:
        in_specs += [pl.BlockSpec((None, gw, d), lambda bi, i: (bi, jnp.maximum(i * r - 1, 0), 0)),
                     pl.BlockSpec((None, gw, d), lambda bi, i: (bi, jnp.minimum((i + 1) * r, ng - 1), 0))]
        args += [x, x]
    in_specs += [pl.BlockSpec((1, d), const),
                 pl.BlockSpec((None, 6, d), lambda bi, i: (bi, 0, 0)),
                 pl.BlockSpec((d, 2 * FFN_DIM), const, pipeline_mode=single),
                 pl.BlockSpec((9, FFN_DIM), const),
                 pl.BlockSpec((1, FFN_DIM), const),
                 pl.BlockSpec((FFN_DIM, d), const, pipeline_mode=single),
                 pl.BlockSpec((1, d), const)]
    args += [nw, mod, wu, cw, cb, wd, fnw]
    kern = functools.partial(_ffn_kernel, tm=tm, gw=gw, halo=halo, final_norm=final_norm)
    return pl.pallas_call(
        kern,
        grid=(b, l // tm),
        in_specs=in_specs,
        out_specs=pl.BlockSpec((None, tm, d), lambda bi, i: (bi, i, 0)),
        out_shape=jax.ShapeDtypeStruct((b, l, d), F32),
        scratch_shapes=[pltpu.VMEM((ext, d), BF16),
                        pltpu.VMEM((ext + 2 * SUBLANES, FCH), F32),
                        pltpu.VMEM((tm, d), F32)],
        compiler_params=_cparams(("arbitrary", "arbitrary")),
        name="ffn",
    )(*args)


def _gate_weights(wa, wx):
    hpb = CBLK // LRU_HEAD_DIM
    nblk = LRU_WIDTH // CBLK
    eye = jnp.eye(hpb, dtype=F32)

    def blockdiag(w):
        w = w.reshape(nblk, hpb, LRU_HEAD_DIM, LRU_HEAD_DIM)
        bd = jnp.einsum('bhij,hk->bhikj', w, eye)
        return bd.reshape(nblk, CBLK, CBLK)

    parts = [blockdiag(wa[0]), blockdiag(wx[0]), blockdiag(wa[1]), blockdiag(wx[1])]
    return jnp.concatenate(parts, axis=-1).astype(BF16)


def _gate_biases(ba, bx):
    nblk = LRU_WIDTH // CBLK
    parts = [v.reshape(nblk, 1, CBLK) for v in (ba[0], bx[0], ba[1], bx[1])]
    return jnp.concatenate(parts, axis=-1)


def kernel(x, c, ctx, c_ctx, ada_w, ada_b, norm_mix_w, norm_ffn_w, w_in, lru_conv_w, lru_conv_b, lru_wa, lru_ba, lru_wx, lru_bx, lru_lambda, lru_proj, ssd_conv_w, ssd_conv_b, ssd_dt_bias, ssd_a_log, ssd_d, ssd_norm_w, ssd_proj, w_out, ffn_w_up, ffn_conv_w, ffn_conv_b, ffn_w_down, final_norm_w):
    b, l, d = x.shape
    lc = ctx.shape[1]
    depth = ada_w.shape[0]
    gw_x = 64
    assert d == D_MODEL and b + 1 <= 16 and l % Q == 0 and lc % Q == 0

    s = jnp.zeros((16, d), F32).at[:b].set(c).at[b].set(c_ctx)
    mods = _ada_call(s, ada_w, ada_b)

    o_lx, o_lg, o_z, o_xbc, o_dt, o_gt = 0, 1024, 2048, 4096, 7168, 7232
    h0 = jnp.zeros((b, 2, LRU_WIDTH), F32)
    s0 = jnp.zeros((b, 2, SSD_STATE, SSD_INNER), F32)
    fnw = final_norm_w.reshape(1, d)

    for li in range(depth):
        last = li == depth - 1
        modx = mods[li, :b].reshape(b, 6, d)
        modc = jnp.broadcast_to(mods[li, b].reshape(1, 6, d), (b, 6, d))
        wl = w_in[li]
        w_big = jnp.concatenate([wl[:, o_z:o_xbc], wl[:, o_gt:], wl[:, o_lx:o_z], wl[:, o_xbc:o_dt]],
                                axis=1).astype(BF16)
        w_dt = jnp.concatenate([wl[:, o_dt:o_gt], wl[:, o_dt:o_gt]], axis=1).astype(BF16)
        nmw = norm_mix_w[li].reshape(1, d)
        nfw = norm_ffn_w[li].reshape(1, d)
        wg = _gate_weights(lru_wa[li], lru_wx[li])
        bgs = _gate_biases(lru_ba[li], lru_bx[li])
        lcw, lcb = lru_conv_w[li], lru_conv_b[li].reshape(1, -1)
        scw, scb = ssd_conv_w[li], ssd_conv_b[li].reshape(1, -1)
        dtb = jnp.tile(ssd_dt_bias[li].reshape(1, -1), (1, 2))
        a128 = jnp.tile(-jnp.exp(ssd_a_log[li].astype(F32)).reshape(1, -1), (1, 2))
        dsk = jnp.concatenate([jnp.zeros((1, DT_W - SSD_HEADS), F32), ssd_d[li].reshape(1, -1)], axis=1)
        snw = ssd_norm_w[li].reshape(1, -1)
        wlp = lru_proj[li].astype(BF16)
        wsp = ssd_proj[li].astype(BF16)
        wo = w_out[li].astype(BF16)
        wu = ffn_w_up[li].astype(BF16)
        wd = ffn_w_down[li].astype(BF16)
        fcw = ffn_conv_w[li].reshape(9, FFN_DIM)
        fcb = ffn_conv_b[li].reshape(1, FFN_DIM)

        big_c, dt_c = _inproj_call(ctx, nmw, modc, w_big, w_dt)
        ra_c, hfin_c = _lru_call(big_c, lcw, lcb, wg, bgs, lru_lambda[li], h0)
        xact_c, dtsp_c = _ssdprep_call(big_c, dt_c, scw, scb, dtb)
        sef_c, seb_c, sfin_c = _ssdstate_call(xact_c, dtsp_c, a128, s0)

        big_x, dt_x = _inproj_call(x, nmw, modx, w_big, w_dt)
        ra_x, _ = _lru_call(big_x, lcw, lcb, wg, bgs, lru_lambda[li], hfin_c)
        xact_x, dtsp_x = _ssdprep_call(big_x, dt_x, scw, scb, dtb)
        sef_x, seb_x, _ = _ssdstate_call(xact_x, dtsp_x, a128, sfin_c)
        gn_x = _ssdy_call(xact_x, dtsp_x, a128, dsk, sef_x, seb_x, big_x, snw)
        x = _merge_call(ra_x, gn_x, big_x, x, modx, wlp, wsp, wo)
        x = _ffn_call(x, nfw, modx, wu, fcw, fcb, wd, fnw, gw=gw_x, final_norm=last)

        if not last:
            gn_c = _ssdy_call(xact_c, dtsp_c, a128, dsk, sef_c, seb_c, big_c, snw)
            ctx = _merge_call(ra_c, gn_c, big_c, ctx, modc, wlp, wsp, wo)
            ctx = _ffn_call(ctx, nfw, modc, wu, fcw, fcb, wd, fnw, gw=lc, final_norm=False)
    return x
```

```python
import functools

import jax
import jax.numpy as jnp
from jax import lax
from jax.experimental import pallas as pl
from jax.experimental.pallas import tpu as pltpu

F32 = jnp.float32
BF16 = jnp.bfloat16

EPS = 1e-6
D_MODEL = 1024
LRU_WIDTH = 1024
LRU_HEADS = 16
LRU_HEAD_DIM = 64
LRU_C = 8.0
SSD_INNER = 2048
SSD_HEADS = 32
SSD_HEAD_DIM = 64
SSD_GROUPS = 4
SSD_STATE = 128
SSD_CONV_DIM = SSD_INNER + 2 * SSD_GROUPS * SSD_STATE
SSD_GROUP_W = SSD_INNER // SSD_GROUPS
FFN_DIM = 2816

LANES = 128
SUBLANES = 8
VMEM_LIMIT_BYTES = 56 * 1024 * 1024

OFF_Z = 0
OFF_GT = OFF_Z + SSD_INNER
OFF_LX = OFF_GT + 2 * D_MODEL
OFF_LG = OFF_LX + LRU_WIDTH
OFF_XBC = OFF_LG + LRU_WIDTH
BIG_W = OFF_XBC + SSD_CONV_DIM
DT_W = 128

CBLK = 256
CONV_HALO = 16
Q = 128
FCH = 256
assert Q == LANES and (FFN_DIM // FCH) % 2 == 1


def _cparams(sem):
    return pltpu.CompilerParams(dimension_semantics=sem, vmem_limit_bytes=VMEM_LIMIT_BYTES)


def _gelu_tanh(x):
    return 0.5 * x * (1.0 + jnp.tanh(0.7978845608028654 * (x + 0.044715 * (x * x * x))))


def _sigmoid(x):
    return 1.0 / (1.0 + jnp.exp(-x))


def _norm_mod(x, nw, shift, scale):
    ms = jnp.mean(x * x, axis=-1, keepdims=True)
    y = x * lax.rsqrt(ms + EPS) * nw
    return y * (1.0 + scale) + shift


def _pair_expand(v, j, lane_lo, off=0):
    return jnp.where(lane_lo, v[:, off + 2 * j:off + 2 * j + 1], v[:, off + 2 * j + 1:off + 2 * j + 2])


def _tri_dot(tri_bf16, v):
    hi = v.astype(BF16)
    r1 = v - hi.astype(F32)
    mid = r1.astype(BF16)
    lo = (r1 - mid.astype(F32)).astype(BF16)
    return (jnp.dot(tri_bf16, hi, preferred_element_type=F32)
            + jnp.dot(tri_bf16, mid, preferred_element_type=F32)
            + jnp.dot(tri_bf16, lo, preferred_element_type=F32))


def _ada_kernel(s_ref, w_ref, b_ref, o_ref):
    s = s_ref[...]
    s = s * _sigmoid(s)
    o_ref[...] = jnp.dot(s, w_ref[...], preferred_element_type=F32,
                         precision=lax.Precision.HIGHEST) + b_ref[...]


def _ada_call(s, ada_w, ada_b):
    depth, d, n = ada_w.shape
    tn = 1536
    rows = s.shape[0]
    return pl.pallas_call(
        _ada_kernel,
        grid=(depth, n // tn),
        in_specs=[pl.BlockSpec((rows, d), lambda l, j: (0, 0)),
                  pl.BlockSpec((None, d, tn), lambda l, j: (l, 0, j)),
                  pl.BlockSpec((None, 1, tn), lambda l, j: (l, 0, j))],
        out_specs=pl.BlockSpec((None, rows, tn), lambda l, j: (l, 0, j)),
        out_shape=jax.ShapeDtypeStruct((depth, rows, n), F32),
        compiler_params=_cparams(("arbitrary", "arbitrary")),
        name="ada",
    )(s, ada_w, ada_b.reshape(depth, 1, n))


def _inproj_kernel(x_ref, nw_ref, mod_ref, w_ref, wdt_ref, big_ref, dt_ref, h_scr):
    @pl.when(pl.program_id(2) == 0)
    def _():
        h = _norm_mod(x_ref[...], nw_ref[...], mod_ref[0:1, :], mod_ref[1:2, :]).astype(BF16)
        h_scr[...] = h
        dt_ref[...] = jnp.dot(h, wdt_ref[...], preferred_element_type=F32)

    big_ref[...] = jnp.dot(h_scr[...], w_ref[...], preferred_element_type=F32).astype(big_ref.dtype)


def _inproj_call(x, nw, mod, w, wdt):
    b, l, d = x.shape
    tm = min(2048, l)
    tn = 1024
    return pl.pallas_call(
        _inproj_kernel,
        grid=(b, l // tm, BIG_W // tn),
        in_specs=[pl.BlockSpec((None, tm, d), lambda bi, i, j: (bi, i, 0)),
                  pl.BlockSpec((1, d), lambda bi, i, j: (0, 0)),
                  pl.BlockSpec((None, 6, d), lambda bi, i, j: (bi, 0, 0)),
                  pl.BlockSpec((d, tn), lambda bi, i, j: (0, j)),
                  pl.BlockSpec((d, DT_W), lambda bi, i, j: (0, 0))],
        out_specs=[pl.BlockSpec((None, tm, tn), lambda bi, i, j: (bi, i, j)),
                   pl.BlockSpec((None, tm, DT_W), lambda bi, i, j: (bi, i, 0))],
        out_shape=[jax.ShapeDtypeStruct((b, l, BIG_W), BF16),
                   jax.ShapeDtypeStruct((b, l, DT_W), F32)],
        scratch_shapes=[pltpu.VMEM((tm, d), BF16)],
        compiler_params=_cparams(("arbitrary", "arbitrary", "arbitrary")),
        name="in_proj",
    )(x, nw, mod, w, wdt)


def _conv4_tile(x_ref, cw_ref, cb_ref, i, n_tiles, t, l):
    hr = CONV_HALO
    r0 = pl.multiple_of(i * t, t)
    main = x_ref[pl.ds(r0, t), :].astype(F32)
    prev = x_ref[pl.ds(pl.multiple_of(jnp.maximum(r0 - hr, 0), hr), hr), :].astype(F32)
    nxt = x_ref[pl.ds(pl.multiple_of(jnp.minimum(r0 + t, l - hr), hr), hr), :].astype(F32)
    prev = jnp.where(i > 0, prev, 0.0)
    nxt = jnp.where(i < n_tiles - 1, nxt, 0.0)
    win = jnp.concatenate([prev, main, nxt], axis=0)
    acc = cb_ref[...] + cw_ref[0:1, :] * win[hr - 2:hr - 2 + t]
    for k in range(1, 4):
        acc = acc + cw_ref[k:k + 1, :] * win[hr - 2 + k:hr - 2 + k + t]
    return acc


def _lru_kernel(lx_ref, lg_ref, cw_ref, cb_ref, wg_ref, bg_ref, lam_ref, h0_ref,
                o_ref, hfin_ref, u_scr, y_scr, a_scr, b_scr, *, l, t):
    n_tiles = l // t
    n_grp = t // SUBLANES
    c = CBLK
    lam = lam_ref[...]
    c8 = LRU_C * (jnp.minimum(lam, 0.0) - jnp.log1p(jnp.exp(-jnp.abs(lam))))
    row8 = lax.broadcasted_iota(jnp.int32, (SUBLANES, c), 0)

    def gates(u, d):
        g = jnp.dot(u.astype(BF16), wg_ref[:, d * 2 * c:(d + 1) * 2 * c],
                    preferred_element_type=F32) + bg_ref[:, d * 2 * c:(d + 1) * 2 * c]
        r = _sigmoid(g[:, :c])
        ig = _sigmoid(g[:, c:])
        log_a = c8[d:d + 1, :] * r
        a = jnp.exp(log_a)
        bb = jnp.sqrt(-jnp.tanh(log_a) * (a * a + 1.0)) * (ig * u)
        a_scr[...] = a
        b_scr[...] = bb

    def fwd_tile(i, hprev):
        r0 = pl.multiple_of(i * t, t)
        u = _conv4_tile(lx_ref, cw_ref, cb_ref, i, n_tiles, t, l)
        u_scr[pl.ds(r0, t), :] = u
        gates(u, 0)

        def grp(g, hp):
            r = pl.multiple_of(g * SUBLANES, SUBLANES)
            av = a_scr[pl.ds(r, SUBLANES), :]
            bv = b_scr[pl.ds(r, SUBLANES), :]
            for k in (1, 2, 4):
                a_s = pltpu.roll(av, k, 0)
                b_s = pltpu.roll(bv, k, 0)
                m = row8 >= k
                bv = jnp.where(m, av * b_s + bv, bv)
                av = jnp.where(m, av * a_s, av)
            h = bv + av * hp
            y_scr[pl.ds(r0 + r, SUBLANES), :] = h
            return h[SUBLANES - 1:SUBLANES, :]

        return lax.fori_loop(0, n_grp, grp, hprev, unroll=8)

    hf = lax.fori_loop(0, n_tiles, fwd_tile, h0_ref[0:1, :])
    hfin_ref[0:1, :] = hf

    def bwd_tile(ii, hnext):
        i = n_tiles - 1 - ii
        r0 = pl.multiple_of(i * t, t)
        u = u_scr[pl.ds(r0, t), :]
        gates(u, 1)

        def grp(gg, hn):
            g = n_grp - 1 - gg
            r = pl.multiple_of(g * SUBLANES, SUBLANES)
            av = a_scr[pl.ds(r, SUBLANES), :]
            bv = b_scr[pl.ds(r, SUBLANES), :]
            for k in (1, 2, 4):
                a_s = pltpu.roll(av, SUBLANES - k, 0)
                b_s = pltpu.roll(bv, SUBLANES - k, 0)
                m = row8 < SUBLANES - k
                bv = jnp.where(m, av * b_s + bv, bv)
                av = jnp.where(m, av * a_s, av)
            h = bv + av * hn
            y_scr[pl.ds(r0 + r, SUBLANES), :] = y_scr[pl.ds(r0 + r, SUBLANES), :] + h
            return h[0:1, :]

        hn = lax.fori_loop(0, n_grp, grp, hnext, unroll=8)
        o_ref[pl.ds(r0, t), :] = (y_scr[pl.ds(r0, t), :] * _gelu_tanh(lg_ref[pl.ds(r0, t), :].astype(F32))).astype(BF16)
        return hn

    hb = lax.fori_loop(0, n_tiles, bwd_tile, h0_ref[1:2, :])
    hfin_ref[1:2, :] = hb


def _lru_call(big, cw, cb, wg, bg, lam, h0):
    b, l, _ = big.shape
    t = 256
    nblk = LRU_WIDTH // CBLK
    lx0 = OFF_LX // CBLK
    lg0 = OFF_LG // CBLK
    kern = functools.partial(_lru_kernel, l=l, t=t)
    return pl.pallas_call(
        kern,
        grid=(b, nblk),
        in_specs=[pl.BlockSpec((None, l, CBLK), lambda bi, ci: (bi, 0, lx0 + ci)),
                  pl.BlockSpec((None, l, CBLK), lambda bi, ci: (bi, 0, lg0 + ci)),
                  pl.BlockSpec((4, CBLK), lambda bi, ci: (0, ci)),
                  pl.BlockSpec((1, CBLK), lambda bi, ci: (0, ci)),
                  pl.BlockSpec((None, CBLK, 4 * CBLK), lambda bi, ci: (ci, 0, 0)),
                  pl.BlockSpec((None, 1, 4 * CBLK), lambda bi, ci: (ci, 0, 0)),
                  pl.BlockSpec((2, CBLK), lambda bi, ci: (0, ci)),
                  pl.BlockSpec((None, 2, CBLK), lambda bi, ci: (bi, 0, ci))],
        out_specs=[pl.BlockSpec((None, l, CBLK), lambda bi, ci: (bi, 0, ci)),
                   pl.BlockSpec((None, 2, CBLK), lambda bi, ci: (bi, 0, ci))],
        out_shape=[jax.ShapeDtypeStruct((b, l, LRU_WIDTH), BF16),
                   jax.ShapeDtypeStruct((b, 2, LRU_WIDTH), F32)],
        scratch_shapes=[pltpu.VMEM((l, CBLK), F32), pltpu.VMEM((l, CBLK), F32),
                        pltpu.VMEM((t, CBLK), F32), pltpu.VMEM((t, CBLK), F32)],
        compiler_params=_cparams(("arbitrary", "arbitrary")),
        name="lru",
    )(big, big, cw, cb, wg, bg, lam, h0)


def _ssdprep_kernel(x_ref, cw_ref, cb_ref, dt_ref, dtb_ref, o_ref, dto_ref, *, l, t):
    n_tiles = l // t

    def tile(i, carry):
        r0 = pl.multiple_of(i * t, t)
        y = _conv4_tile(x_ref, cw_ref, cb_ref, i, n_tiles, t, l)
        o_ref[pl.ds(r0, t), :] = (y * _sigmoid(y)).astype(BF16)
        return carry

    lax.fori_loop(0, n_tiles, tile, 0)

    @pl.when(pl.program_id(1) == 0)
    def _():
        v = dt_ref[...] + dtb_ref[...]
        dto_ref[...] = jnp.maximum(v, 0.0) + jnp.log1p(jnp.exp(-jnp.abs(v)))


def _ssdprep_call(big, dt, cw, cb, dtb):
    b, l, _ = big.shape
    t = 256
    nblk = SSD_CONV_DIM // CBLK
    x0 = OFF_XBC // CBLK
    kern = functools.partial(_ssdprep_kernel, l=l, t=t)
    return pl.pallas_call(
        kern,
        grid=(b, nblk),
        in_specs=[pl.BlockSpec((None, l, CBLK), lambda bi, ci: (bi, 0, x0 + ci)),
                  pl.BlockSpec((4, CBLK), lambda bi, ci: (0, ci)),
                  pl.BlockSpec((1, CBLK), lambda bi, ci: (0, ci)),
                  pl.BlockSpec((None, l, DT_W), lambda bi, ci: (bi, 0, 0)),
                  pl.BlockSpec((1, DT_W), lambda bi, ci: (0, 0))],
        out_specs=[pl.BlockSpec((None, l, CBLK), lambda bi, ci: (bi, 0, ci)),
                   pl.BlockSpec((None, l, DT_W), lambda bi, ci: (bi, 0, 0))],
        out_shape=[jax.ShapeDtypeStruct((b, l, SSD_CONV_DIM), BF16),
                   jax.ShapeDtypeStruct((b, l, DT_W), F32)],
        compiler_params=_cparams(("arbitrary", "arbitrary")),
        name="ssd_prep",
    )(big, cw, cb, dt, dtb)


def _ssdstate_kernel(xf_ref, xb_ref, dtf_ref, dtb_ref, a_ref, s0_ref,
                     sef_ref, seb_ref, sfin_ref, sf_scr, sb_scr, xw_scr):
    c = pl.program_id(1)
    nc = pl.num_programs(1)

    @pl.when(c == 0)
    def _():
        sf_scr[...] = s0_ref[0]
        sb_scr[...] = s0_ref[1]

    a = a_ref[...]
    row = lax.broadcasted_iota(jnp.int32, (Q, Q), 0)
    col = lax.broadcasted_iota(jnp.int32, (Q, Q), 1)
    lane_lo = lax.broadcasted_iota(jnp.int32, (1, LANES), 1) < SSD_HEAD_DIM
    hp = SSD_HEADS // 2
    gp = hp // SSD_GROUPS

    for d in range(2):
        x_ref = (xf_ref, xb_ref)[d]
        dt = (dtf_ref, dtb_ref)[d][...]
        off = d * SSD_HEADS
        s_scr = (sf_scr, sb_scr)[d]
        se_ref = (sef_ref, seb_ref)[d]
        tri = jnp.where(col <= row, 1.0, 0.0) if d == 0 else jnp.where(col >= row, 1.0, 0.0)
        cs = _tri_dot(tri.astype(BF16), dt * a)
        tot = cs[Q - 1:Q, :] if d == 0 else cs[0:1, :]
        w = jnp.exp(tot - cs) * dt
        dec = jnp.exp(tot)
        se_ref[...] = s_scr[...].astype(BF16)
        for j in range(hp):
            sl = slice(j * LANES, (j + 1) * LANES)
            xw_scr[:, sl] = (x_ref[:, sl].astype(F32) * _pair_expand(w, j, lane_lo, off)).astype(BF16)
        for g in range(SSD_GROUPS):
            bg = x_ref[:, SSD_INNER + g * SSD_STATE:SSD_INNER + (g + 1) * SSD_STATE]
            sloc = lax.dot_general(bg, xw_scr[:, g * SSD_GROUP_W:(g + 1) * SSD_GROUP_W],
                                   (((0,), (0,)), ((), ())), preferred_element_type=F32)
            for jj in range(gp):
                j = g * gp + jj
                sl = slice(j * LANES, (j + 1) * LANES)
                s_scr[:, sl] = (s_scr[:, sl] * _pair_expand(dec, j, lane_lo, off)
                                + sloc[:, jj * LANES:(jj + 1) * LANES])

    @pl.when(c == nc - 1)
    def _():
        sfin_ref[0] = sf_scr[...]
        sfin_ref[1] = sb_scr[...]


def _ssdstate_call(xact, dtsp, a128, s0):
    b, l, _ = xact.shape
    nc = l // Q
    return pl.pallas_call(
        _ssdstate_kernel,
        grid=(b, nc),
        in_specs=[pl.BlockSpec((None, Q, SSD_CONV_DIM), lambda bi, ci: (bi, ci, 0)),
                  pl.BlockSpec((None, Q, SSD_CONV_DIM), lambda bi, ci: (bi, nc - 1 - ci, 0)),
                  pl.BlockSpec((None, Q, DT_W), lambda bi, ci: (bi, ci, 0)),
                  pl.BlockSpec((None, Q, DT_W), lambda bi, ci: (bi, nc - 1 - ci, 0)),
                  pl.BlockSpec((1, DT_W), lambda bi, ci: (0, 0)),
                  pl.BlockSpec((None, 2, SSD_STATE, SSD_INNER), lambda bi, ci: (bi, 0, 0, 0))],
        out_specs=[pl.BlockSpec((None, None, SSD_STATE, SSD_INNER), lambda bi, ci: (bi, ci, 0, 0)),
                   pl.BlockSpec((None, None, SSD_STATE, SSD_INNER), lambda bi, ci: (bi, nc - 1 - ci, 0, 0)),
                   pl.BlockSpec((None, 2, SSD_STATE, SSD_INNER), lambda bi, ci: (bi, 0, 0, 0))],
        out_shape=[jax.ShapeDtypeStruct((b, nc, SSD_STATE, SSD_INNER), BF16),
                   jax.ShapeDtypeStruct((b, nc, SSD_STATE, SSD_INNER), BF16),
                   jax.ShapeDtypeStruct((b, 2, SSD_STATE, SSD_INNER), F32)],
        scratch_shapes=[pltpu.VMEM((SSD_STATE, SSD_INNER), F32), pltpu.VMEM((SSD_STATE, SSD_INNER), F32),
                        pltpu.VMEM((Q, SSD_INNER), BF16)],
        compiler_params=_cparams(("arbitrary", "arbitrary")),
        name="ssd_states",
    )(xact, xact, dtsp, dtsp, a128, s0)


def _ssdy_kernel(x_ref, dt_ref, a_ref, dsk_ref, sef_ref, seb_ref, z_ref, nw_ref, o_ref, y_scr):
    nh = SSD_HEADS
    row = lax.broadcasted_iota(jnp.int32, (Q, Q), 0)
    col = lax.broadcasted_iota(jnp.int32, (Q, Q), 1)
    low = col <= row
    lane = lax.broadcasted_iota(jnp.int32, (1, LANES), 1)
    lane_lo = lane < SSD_HEAD_DIM
    dt = dt_ref[...]
    da = dt * a_ref[...]
    cs_f = _tri_dot(jnp.where(low, 1.0, 0.0).astype(BF16), da)
    cs_b = _tri_dot(jnp.where(col >= row, 1.0, 0.0).astype(BF16), da)
    p = jnp.where(lane < nh, cs_f, jnp.where(lane < 2 * nh, cs_b, dt))
    pt = p.T
    eye = col == row
    hp_per_g = nh // 2 // SSD_GROUPS

    for g in range(SSD_GROUPS):
        bg = x_ref[:, SSD_INNER + g * SSD_STATE:SSD_INNER + (g + 1) * SSD_STATE]
        cg = x_ref[:, SSD_INNER + (SSD_GROUPS + g) * SSD_STATE:SSD_INNER + (SSD_GROUPS + g + 1) * SSD_STATE]
        cb = lax.dot_general(cg, bg, (((1,), (1,)), ((), ())), preferred_element_type=F32)
        gsl = slice(g * SSD_GROUP_W, (g + 1) * SSD_GROUP_W)
        yoff_f = jnp.dot(cg, sef_ref[:, gsl], preferred_element_type=F32)
        yoff_b = jnp.dot(cg, seb_ref[:, gsl], preferred_element_type=F32)
        for jj in range(hp_per_g):
            j = g * hp_per_g + jj
            sl = slice(j * LANES, (j + 1) * LANES)
            xp = x_ref[:, sl]
            res, cfs, cbs = [], [], []
            for h in (2 * j, 2 * j + 1):
                cf = jnp.broadcast_to(p[:, h:h + 1], (Q, Q))
                cbk = jnp.broadcast_to(p[:, nh + h:nh + h + 1], (Q, Q))
                dtf_r = pt[2 * nh + h:2 * nh + h + 1, :]
                dtb_r = pt[3 * nh + h:3 * nh + h + 1, :]
                arg = jnp.where(low, cf - pt[h:h + 1, :], cbk - pt[nh + h:nh + h + 1, :])
                dts = jnp.where(eye, dtf_r + dtb_r, jnp.where(low, dtf_r, dtb_r))
                m = cb * (jnp.exp(arg) * dts) + jnp.where(eye, dsk_ref[h], 0.0)
                res.append(jnp.dot(m.astype(BF16), xp, preferred_element_type=F32))
                cfs.append(cf)
                cbs.append(cbk)
            sc_f = jnp.exp(jnp.where(lane_lo, cfs[0], cfs[1]))
            sc_b = jnp.exp(jnp.where(lane_lo, cbs[0], cbs[1]))
            y = (jnp.where(lane_lo, res[0], res[1])
                 + sc_f * yoff_f[:, jj * LANES:(jj + 1) * LANES]
                 + sc_b * yoff_b[:, jj * LANES:(jj + 1) * LANES])
            y_scr[:, sl] = y
        yg = y_scr[:, gsl]
        zg = z_ref[:, gsl].astype(F32)
        gated = yg * (zg * _sigmoid(zg))
        ms = jnp.mean(gated * gated, axis=-1, keepdims=True)
        o_ref[:, gsl] = (gated * lax.rsqrt(ms + EPS) * nw_ref[:, gsl]).astype(BF16)


def _ssdy_call(xact, dtsp, a128, dskip, sef, seb, big, nw):
    b, l, _ = xact.shape
    nc = l // Q
    return pl.pallas_call(
        _ssdy_kernel,
        grid=(b, nc),
        in_specs=[pl.BlockSpec((None, Q, SSD_CONV_DIM), lambda bi, ci: (bi, ci, 0)),
                  pl.BlockSpec((None, Q, DT_W), lambda bi, ci: (bi, ci, 0)),
                  pl.BlockSpec((1, DT_W), lambda bi, ci: (0, 0)),
                  pl.BlockSpec(memory_space=pltpu.SMEM),
                  pl.BlockSpec((None, None, SSD_STATE, SSD_INNER), lambda bi, ci: (bi, ci, 0, 0)),
                  pl.BlockSpec((None, None, SSD_STATE, SSD_INNER), lambda bi, ci: (bi, ci, 0, 0)),
                  pl.BlockSpec((None, Q, SSD_INNER), lambda bi, ci: (bi, ci, OFF_Z // SSD_INNER)),
                  pl.BlockSpec((1, SSD_INNER), lambda bi, ci: (0, 0))],
        out_specs=pl.BlockSpec((None, Q, SSD_INNER), lambda bi, ci: (bi, ci, 0)),
        out_shape=jax.ShapeDtypeStruct((b, l, SSD_INNER), BF16),
        scratch_shapes=[pltpu.VMEM((Q, SSD_INNER), F32)],
        compiler_params=_cparams(("arbitrary", "arbitrary")),
        name="ssd_y",
    )(xact, dtsp, a128, dskip, sef, seb, big, nw)


def _merge_kernel(ra_ref, gn_ref, gt_ref, x_ref, mod_ref, wl_ref, ws_ref, wo_ref, o_ref):
    ra = jnp.dot(ra_ref[...], wl_ref[...], preferred_element_type=F32)
    rb = jnp.dot(gn_ref[...], ws_ref[...], preferred_element_type=F32)
    ga = _sigmoid(gt_ref[:, :D_MODEL].astype(F32))
    gb = _sigmoid(gt_ref[:, D_MODEL:].astype(F32))
    m = (ga * ra + gb * rb).astype(BF16)
    o = jnp.dot(m, wo_ref[...], preferred_element_type=F32)
    o_ref[...] = x_ref[...] + mod_ref[2:3, :] * o


def _merge_call(ra, gn, big, x, mod, wl, ws, wo):
    b, l, d = x.shape
    tm = min(512, l)
    const = lambda bi, i: (0, 0)
    return pl.pallas_call(
        _merge_kernel,
        grid=(b, l // tm),
        in_specs=[pl.BlockSpec((None, tm, LRU_WIDTH), lambda bi, i: (bi, i, 0)),
                  pl.BlockSpec((None, tm, SSD_INNER), lambda bi, i: (bi, i, 0)),
                  pl.BlockSpec((None, tm, 2 * d), lambda bi, i: (bi, i, OFF_GT // (2 * D_MODEL))),
                  pl.BlockSpec((None, tm, d), lambda bi, i: (bi, i, 0)),
                  pl.BlockSpec((None, 6, d), lambda bi, i: (bi, 0, 0)),
                  pl.BlockSpec((LRU_WIDTH, d), const),
                  pl.BlockSpec((SSD_INNER, d), const),
                  pl.BlockSpec((d, d), const)],
        out_specs=pl.BlockSpec((None, tm, d), lambda bi, i: (bi, i, 0)),
        out_shape=jax.ShapeDtypeStruct((b, l, d), F32),
        compiler_params=_cparams(("arbitrary", "arbitrary")),
        name="merge",
    )(ra, gn, big, x, mod, wl, ws, wo)


def _ffn_kernel(*refs, tm, gw, halo, final_norm):
    if halo:
        (x_ref, xp_ref, xn_ref, nw_ref, mod_ref, wu_ref, cw_ref, cb_ref, wd_ref, fnw_ref,
         o_ref, h_scr, u_scr, v_scr, acc_scr) = refs
    else:
        (x_ref, nw_ref, mod_ref, wu_ref, cw_ref, cb_ref, wd_ref, fnw_ref,
         o_ref, h_scr, u_scr, v_scr, acc_scr) = refs
    i = pl.program_id(1)
    n_i = pl.num_programs(1)
    nw = nw_ref[...]
    shift = mod_ref[3:4, :]
    scale = mod_ref[4:5, :]
    ext = tm + 2 * gw if halo else tm
    top = gw if halo else 0
    pad = SUBLANES

    x = x_ref[...]
    h_scr[top:top + tm, :] = _norm_mod(x, nw, shift, scale).astype(BF16)
    if halo:
        hp = _norm_mod(xp_ref[...], nw, shift, scale)
        hn = _norm_mod(xn_ref[...], nw, shift, scale)
        h_scr[0:gw, :] = jnp.where(i > 0, hp, 0.0).astype(BF16)
        h_scr[gw + tm:ext, :] = jnp.where(i < n_i - 1, hn, 0.0).astype(BF16)
    for slot in range(2):
        u_scr[slot, 0:pad, :] = jnp.zeros((pad, FCH), F32)
        u_scr[slot, pad + ext:pad + ext + pad, :] = jnp.zeros((pad, FCH), F32)
    acc_scr[...] = jnp.zeros_like(acc_scr)

    def up(ci, slot):
        c0 = pl.multiple_of(ci * FCH, FCH)
        u_scr[slot, pad:pad + ext, :] = jnp.dot(h_scr[...], wu_ref[:, pl.ds(c0, FCH)],
                                                preferred_element_type=F32)
        v_scr[slot] = jnp.dot(h_scr[top:top + tm, :], wu_ref[:, pl.ds(FFN_DIM + c0, FCH)],
                              preferred_element_type=F32)

    def down(ci, slot):
        c0 = pl.multiple_of(ci * FCH, FCH)
        colid = lax.broadcasted_iota(jnp.int32, (ext, FCH), 0) & (gw - 1)
        u = u_scr[slot, pad:pad + ext, :]
        ul = jnp.where(colid >= 1, u_scr[slot, pad - 1:pad - 1 + ext, :], 0.0)
        ur = jnp.where(colid <= gw - 2, u_scr[slot, pad + 1:pad + 1 + ext, :], 0.0)
        cw = cw_ref[:, pl.ds(c0, FCH)]
        acc = cb_ref[:, pl.ds(c0, FCH)]
        for dr in ((-1, 0, 1) if halo else (0,)):
            o = top + dr * gw
            k = (dr + 1) * 3
            acc = (acc + cw[k:k + 1, :] * ul[o:o + tm] + cw[k + 1:k + 2, :] * u[o:o + tm]
                   + cw[k + 2:k + 3, :] * ur[o:o + tm])
        act = (_gelu_tanh(acc) * v_scr[slot]).astype(BF16)
        acc_scr[...] += jnp.dot(act, wd_ref[pl.ds(c0, FCH), :], preferred_element_type=F32)

    n_ch = FFN_DIM // FCH
    up(0, 0)

    def pair(pi, carry):
        c = 2 * pi
        up(c + 1, 1)
        down(c, 0)
        up(c + 2, 0)
        down(c + 1, 1)
        return carry

    lax.fori_loop(0, (n_ch - 1) // 2, pair, 0)
    down(n_ch - 1, 0)
    y = x + mod_ref[5:6, :] * acc_scr[...]
    if final_norm:
        ms = jnp.mean(y * y, axis=-1, keepdims=True)
        y = y * lax.rsqrt(ms + EPS) * fnw_ref[...]
    o_ref[...] = y


def _ffn_call(x, nw, mod, wu, cw, cb, wd, fnw, *, gw, final_norm):
    b, l, d = x.shape
    rows = l // gw
    halo = rows > 1
    tm = min(512, l)
    assert tm % gw == 0 and gw & (gw - 1) == 0
    ext = tm + 2 * gw if halo else tm
    r = tm // gw
    ng = l // gw
    const = lambda bi, i: (0, 0)
    single = pl.Buffered(1)
    in_specs = [pl.BlockSpec((None, tm, d), lambda bi, i: (bi, i, 0))]
    args = [x]
    if halo:
        in_specs += [pl.BlockSpec((None, gw, d), lambda bi, i: (bi, jnp.maximum(i * r - 1, 0), 0)),
                     pl.BlockSpec((None, gw, d), lambda bi, i: (bi, jnp.minimum((i + 1) * r, ng - 1), 0))]
        args += [x, x]
    in_specs += [pl.BlockSpec((1, d), const),
                 pl.BlockSpec((None, 6, d), lambda bi, i: (bi, 0, 0)),
                 pl.BlockSpec((d, 2 * FFN_DIM), const, pipeline_mode=single),
                 pl.BlockSpec((9, FFN_DIM), const),
                 pl.BlockSpec((1, FFN_DIM), const),
                 pl.BlockSpec((FFN_DIM, d), const, pipeline_mode=single),
                 pl.BlockSpec((1, d), const)]
    args += [nw, mod, wu, cw, cb, wd, fnw]
    kern = functools.partial(_ffn_kernel, tm=tm, gw=gw, halo=halo, final_norm=final_norm)
    return pl.pallas_call(
        kern,
        grid=(b, l // tm),
        in_specs=in_specs,
        out_specs=pl.BlockSpec((None, tm, d), lambda bi, i: (bi, i, 0)),
        out_shape=jax.ShapeDtypeStruct((b, l, d), F32),
        scratch_shapes=[pltpu.VMEM((ext, d), BF16),
                        pltpu.VMEM((2, ext + 2 * SUBLANES, FCH), F32),
                        pltpu.VMEM((2, tm, FCH), F32),
                        pltpu.VMEM((tm, d), F32)],
        compiler_params=_cparams(("arbitrary", "arbitrary")),
        name="ffn",
    )(*args)


def _gate_weights(wa, wx):
    hpb = CBLK // LRU_HEAD_DIM
    nblk = LRU_WIDTH // CBLK
    eye = jnp.eye(hpb, dtype=F32)

    def blockdiag(w):
        w = w.reshape(nblk, hpb, LRU_HEAD_DIM, LRU_HEAD_DIM)
        bd = jnp.einsum('bhij,hk->bhikj', w, eye)
        return bd.reshape(nblk, CBLK, CBLK)

    parts = [blockdiag(wa[0]), blockdiag(wx[0]), blockdiag(wa[1]), blockdiag(wx[1])]
    return jnp.concatenate(parts, axis=-1).astype(BF16)


def _gate_biases(ba, bx):
    nblk = LRU_WIDTH // CBLK
    parts = [v.reshape(nblk, 1, CBLK) for v in (ba[0], bx[0], ba[1], bx[1])]
    return jnp.concatenate(parts, axis=-1)


def kernel(x, c, ctx, c_ctx, ada_w, ada_b, norm_mix_w, norm_ffn_w, w_in, lru_conv_w, lru_conv_b, lru_wa, lru_ba, lru_wx, lru_bx, lru_lambda, lru_proj, ssd_conv_w, ssd_conv_b, ssd_dt_bias, ssd_a_log, ssd_d, ssd_norm_w, ssd_proj, w_out, ffn_w_up, ffn_conv_w, ffn_conv_b, ffn_w_down, final_norm_w):
    b, l, d = x.shape
    lc = ctx.shape[1]
    depth = ada_w.shape[0]
    gw_x = 64
    assert d == D_MODEL and b + 1 <= 16 and l % Q == 0 and lc % Q == 0

    s = jnp.zeros((16, d), F32).at[:b].set(c).at[b].set(c_ctx)
    mods = _ada_call(s, ada_w, ada_b)

    o_lx, o_lg, o_z, o_xbc, o_dt, o_gt = 0, 1024, 2048, 4096, 7168, 7232
    h0 = jnp.zeros((b, 2, LRU_WIDTH), F32)
    s0 = jnp.zeros((b, 2, SSD_STATE, SSD_INNER), F32)
    fnw = final_norm_w.reshape(1, d)

    for li in range(depth):
        last = li == depth - 1
        modx = mods[li, :b].reshape(b, 6, d)
        modc = jnp.broadcast_to(mods[li, b].reshape(1, 6, d), (b, 6, d))
        wl = w_in[li]
        w_big = jnp.concatenate([wl[:, o_z:o_xbc], wl[:, o_gt:], wl[:, o_lx:o_z], wl[:, o_xbc:o_dt]],
                                axis=1).astype(BF16)
        w_dt = jnp.concatenate([wl[:, o_dt:o_gt], wl[:, o_dt:o_gt]], axis=1).astype(BF16)
        nmw = norm_mix_w[li].reshape(1, d)
        nfw = norm_ffn_w[li].reshape(1, d)
        wg = _gate_weights(lru_wa[li], lru_wx[li])
        bgs = _gate_biases(lru_ba[li], lru_bx[li])
        lcw, lcb = lru_conv_w[li], lru_conv_b[li].reshape(1, -1)
        scw, scb = ssd_conv_w[li], ssd_conv_b[li].reshape(1, -1)
        dtb = jnp.tile(ssd_dt_bias[li].reshape(1, -1), (1, 2))
        a128 = jnp.tile(-jnp.exp(ssd_a_log[li].astype(F32)).reshape(1, -1), (1, 2))
        dsk = ssd_d[li].astype(F32)
        snw = ssd_norm_w[li].reshape(1, -1)
        wlp = lru_proj[li].astype(BF16)
        wsp = ssd_proj[li].astype(BF16)
        wo = w_out[li].astype(BF16)
        wu = ffn_w_up[li].astype(BF16)
        wd = ffn_w_down[li].astype(BF16)
        fcw = ffn_conv_w[li].reshape(9, FFN_DIM)
        fcb = ffn_conv_b[li].reshape(1, FFN_DIM)

        big_c, dt_c = _inproj_call(ctx, nmw, modc, w_big, w_dt)
        ra_c, hfin_c = _lru_call(big_c, lcw, lcb, wg, bgs, lru_lambda[li], h0)
        xact_c, dtsp_c = _ssdprep_call(big_c, dt_c, scw, scb, dtb)
        sef_c, seb_c, sfin_c = _ssdstate_call(xact_c, dtsp_c, a128, s0)

        big_x, dt_x = _inproj_call(x, nmw, modx, w_big, w_dt)
        ra_x, _ = _lru_call(big_x, lcw, lcb, wg, bgs, lru_lambda[li], hfin_c)
        xact_x, dtsp_x = _ssdprep_call(big_x, dt_x, scw, scb, dtb)
        sef_x, seb_x, _ = _ssdstate_call(xact_x, dtsp_x, a128, sfin_c)
        gn_x = _ssdy_call(xact_x, dtsp_x, a128, dsk, sef_x, seb_x, big_x, snw)
        x = _merge_call(ra_x, gn_x, big_x, x, modx, wlp, wsp, wo)
        x = _ffn_call(x, nfw, modx, wu, fcw, fcb, wd, fnw, gw=gw_x, final_norm=last)

        if not last:
            gn_c = _ssdy_call(xact_c, dtsp_c, a128, dsk, sef_c, seb_c, big_c, snw)
            ctx = _merge_call(ra_c, gn_c, big_c, ctx, modc, wlp, wsp, wo)
            ctx = _ffn_call(ctx, nfw, modc, wu, fcw, fcb, wd, fnw, gw=lc, final_norm=False)
    return x
```

```python
import functools

import jax
import jax.numpy as jnp
from jax import lax
from jax.experimental import pallas as pl
from jax.experimental.pallas import tpu as pltpu

F32 = jnp.float32
BF16 = jnp.bfloat16

EPS = 1e-6
D_MODEL = 1024
LRU_WIDTH = 1024
LRU_HEADS = 16
LRU_HEAD_DIM = 64
LRU_C = 8.0
SSD_INNER = 2048
SSD_HEADS = 32
SSD_HEAD_DIM = 64
SSD_GROUPS = 4
SSD_STATE = 128
SSD_CONV_DIM = SSD_INNER + 2 * SSD_GROUPS * SSD_STATE
SSD_GROUP_W = SSD_INNER // SSD_GROUPS
FFN_DIM = 2816

LANES = 128
SUBLANES = 8
VMEM_LIMIT_BYTES = 56 * 1024 * 1024

OFF_Z = 0
OFF_GT = OFF_Z + SSD_INNER
OFF_LX = OFF_GT + 2 * D_MODEL
OFF_LG = OFF_LX + LRU_WIDTH
OFF_XBC = OFF_LG + LRU_WIDTH
BIG_W = OFF_XBC + SSD_CONV_DIM
DT_W = 128

CBLK = 256
CONV_HALO = 16
Q = 128
FCH = 256
assert Q == LANES and (FFN_DIM // FCH) % 2 == 1


def _cparams(sem):
    return pltpu.CompilerParams(dimension_semantics=sem, vmem_limit_bytes=VMEM_LIMIT_BYTES)


def _gelu_tanh(x):
    return 0.5 * x * (1.0 + jnp.tanh(0.7978845608028654 * (x + 0.044715 * (x * x * x))))


def _sigmoid(x):
    return 1.0 / (1.0 + jnp.exp(-x))


def _sigmoid_tanh(x):
    return 0.5 * jnp.tanh(0.5 * x) + 0.5


def _norm_mod(x, nw, shift, scale):
    ms = jnp.mean(x * x, axis=-1, keepdims=True)
    y = x * lax.rsqrt(ms + EPS) * nw
    return y * (1.0 + scale) + shift


def _pair_expand(v, j, lane_lo, off=0):
    return jnp.where(lane_lo, v[:, off + 2 * j:off + 2 * j + 1], v[:, off + 2 * j + 1:off + 2 * j + 2])


def _tri_dot(tri_bf16, v):
    hi = v.astype(BF16)
    r1 = v - hi.astype(F32)
    mid = r1.astype(BF16)
    lo = (r1 - mid.astype(F32)).astype(BF16)
    return (jnp.dot(tri_bf16, hi, preferred_element_type=F32)
            + jnp.dot(tri_bf16, mid, preferred_element_type=F32)
            + jnp.dot(tri_bf16, lo, preferred_element_type=F32))


def _ada_kernel(s_ref, w_ref, b_ref, o_ref):
    s = s_ref[...]
    s = s * _sigmoid(s)
    o_ref[...] = jnp.dot(s, w_ref[...], preferred_element_type=F32,
                         precision=lax.Precision.HIGHEST) + b_ref[...]


def _ada_call(s, ada_w, ada_b):
    depth, d, n = ada_w.shape
    tn = 1536
    rows = s.shape[0]
    return pl.pallas_call(
        _ada_kernel,
        grid=(depth, n // tn),
        in_specs=[pl.BlockSpec((rows, d), lambda l, j: (0, 0)),
                  pl.BlockSpec((None, d, tn), lambda l, j: (l, 0, j)),
                  pl.BlockSpec((None, 1, tn), lambda l, j: (l, 0, j))],
        out_specs=pl.BlockSpec((None, rows, tn), lambda l, j: (l, 0, j)),
        out_shape=jax.ShapeDtypeStruct((depth, rows, n), F32),
        compiler_params=_cparams(("arbitrary", "arbitrary")),
        name="ada",
    )(s, ada_w, ada_b.reshape(depth, 1, n))


def _inproj_kernel(x_ref, nw_ref, mod_ref, w_ref, wdt_ref, big_ref, dt_ref, h_scr):
    @pl.when(pl.program_id(2) == 0)
    def _():
        h = _norm_mod(x_ref[...], nw_ref[...], mod_ref[0:1, :], mod_ref[1:2, :]).astype(BF16)
        h_scr[...] = h
        dt_ref[...] = jnp.dot(h, wdt_ref[...], preferred_element_type=F32)

    big_ref[...] = jnp.dot(h_scr[...], w_ref[...], preferred_element_type=F32).astype(big_ref.dtype)


def _inproj_call(x, nw, mod, w, wdt):
    b, l, d = x.shape
    tm = min(2048, l)
    tn = 1536
    return pl.pallas_call(
        _inproj_kernel,
        grid=(b, l // tm, BIG_W // tn),
        in_specs=[pl.BlockSpec((None, tm, d), lambda bi, i, j: (bi, i, 0)),
                  pl.BlockSpec((1, d), lambda bi, i, j: (0, 0)),
                  pl.BlockSpec((None, 6, d), lambda bi, i, j: (bi, 0, 0)),
                  pl.BlockSpec((d, tn), lambda bi, i, j: (0, j)),
                  pl.BlockSpec((d, DT_W), lambda bi, i, j: (0, 0))],
        out_specs=[pl.BlockSpec((None, tm, tn), lambda bi, i, j: (bi, i, j)),
                   pl.BlockSpec((None, tm, DT_W), lambda bi, i, j: (bi, i, 0))],
        out_shape=[jax.ShapeDtypeStruct((b, l, BIG_W), BF16),
                   jax.ShapeDtypeStruct((b, l, DT_W), F32)],
        scratch_shapes=[pltpu.VMEM((tm, d), BF16)],
        compiler_params=_cparams(("arbitrary", "arbitrary", "arbitrary")),
        name="in_proj",
    )(x, nw, mod, w, wdt)


def _conv4_tile(x_ref, cw_ref, cb_ref, i, n_tiles, t, l):
    hr = CONV_HALO
    r0 = pl.multiple_of(i * t, t)
    main = x_ref[pl.ds(r0, t), :].astype(F32)
    prev = x_ref[pl.ds(pl.multiple_of(jnp.maximum(r0 - hr, 0), hr), hr), :].astype(F32)
    nxt = x_ref[pl.ds(pl.multiple_of(jnp.minimum(r0 + t, l - hr), hr), hr), :].astype(F32)
    prev = jnp.where(i > 0, prev, 0.0)
    nxt = jnp.where(i < n_tiles - 1, nxt, 0.0)
    win = jnp.concatenate([prev, main, nxt], axis=0)
    acc = cb_ref[...] + cw_ref[0:1, :] * win[hr - 2:hr - 2 + t]
    for k in range(1, 4):
        acc = acc + cw_ref[k:k + 1, :] * win[hr - 2 + k:hr - 2 + k + t]
    return acc


def _lru_kernel(lx_ref, lg_ref, cw_ref, cb_ref, wg_ref, bg_ref, lam_ref, h0_ref,
                o_ref, hfin_ref, u_scr, y_scr, a_scr, b_scr, *, l, t):
    n_tiles = l // t
    n_grp = t // SUBLANES
    c = CBLK
    lam = lam_ref[...]
    c8 = LRU_C * (jnp.minimum(lam, 0.0) - jnp.log1p(jnp.exp(-jnp.abs(lam))))
    row8 = lax.broadcasted_iota(jnp.int32, (SUBLANES, c), 0)

    def gates(u, d):
        g = jnp.dot(u.astype(BF16), wg_ref[:, d * 2 * c:(d + 1) * 2 * c],
                    preferred_element_type=F32) + bg_ref[:, d * 2 * c:(d + 1) * 2 * c]
        r = _sigmoid_tanh(g[:, :c])
        ig = _sigmoid_tanh(g[:, c:])
        log_a = c8[d:d + 1, :] * r
        a = jnp.exp(log_a)
        s = -jnp.tanh(log_a) * (a * a + 1.0)
        bb = jnp.where(s > 0.0, s * lax.rsqrt(s), 0.0) * (ig * u)
        a_scr[...] = a
        b_scr[...] = bb

    def fwd_tile(i, hprev):
        r0 = pl.multiple_of(i * t, t)
        u = _conv4_tile(lx_ref, cw_ref, cb_ref, i, n_tiles, t, l)
        u_scr[pl.ds(r0, t), :] = u
        gates(u, 0)

        def grp(g, hp):
            r = pl.multiple_of(g * SUBLANES, SUBLANES)
            av = a_scr[pl.ds(r, SUBLANES), :]
            bv = b_scr[pl.ds(r, SUBLANES), :]
            for k in (1, 2, 4):
                a_s = pltpu.roll(av, k, 0)
                b_s = pltpu.roll(bv, k, 0)
                m = row8 >= k
                bv = jnp.where(m, av * b_s + bv, bv)
                av = jnp.where(m, av * a_s, av)
            h = bv + av * hp
            y_scr[pl.ds(r0 + r, SUBLANES), :] = h
            return h[SUBLANES - 1:SUBLANES, :]

        return lax.fori_loop(0, n_grp, grp, hprev, unroll=8)

    hf = lax.fori_loop(0, n_tiles, fwd_tile, h0_ref[0:1, :])
    hfin_ref[0:1, :] = hf

    def bwd_tile(ii, hnext):
        i = n_tiles - 1 - ii
        r0 = pl.multiple_of(i * t, t)
        u = u_scr[pl.ds(r0, t), :]
        gates(u, 1)

        def grp(gg, hn):
            g = n_grp - 1 - gg
            r = pl.multiple_of(g * SUBLANES, SUBLANES)
            av = a_scr[pl.ds(r, SUBLANES), :]
            bv = b_scr[pl.ds(r, SUBLANES), :]
            for k in (1, 2, 4):
                a_s = pltpu.roll(av, SUBLANES - k, 0)
                b_s = pltpu.roll(bv, SUBLANES - k, 0)
                m = row8 < SUBLANES - k
                bv = jnp.where(m, av * b_s + bv, bv)
                av = jnp.where(m, av * a_s, av)
            h = bv + av * hn
            y_scr[pl.ds(r0 + r, SUBLANES), :] = y_scr[pl.ds(r0 + r, SUBLANES), :] + h
            return h[0:1, :]

        hn = lax.fori_loop(0, n_grp, grp, hnext, unroll=8)
        o_ref[pl.ds(r0, t), :] = (y_scr[pl.ds(r0, t), :] * _gelu_tanh(lg_ref[pl.ds(r0, t), :].astype(F32))).astype(BF16)
        return hn

    hb = lax.fori_loop(0, n_tiles, bwd_tile, h0_ref[1:2, :])
    hfin_ref[1:2, :] = hb


def _lru_call(big, cw, cb, wg, bg, lam, h0):
    b, l, _ = big.shape
    t = 256
    nblk = LRU_WIDTH // CBLK
    lx0 = OFF_LX // CBLK
    lg0 = OFF_LG // CBLK
    kern = functools.partial(_lru_kernel, l=l, t=t)
    return pl.pallas_call(
        kern,
        grid=(b, nblk),
        in_specs=[pl.BlockSpec((None, l, CBLK), lambda bi, ci: (bi, 0, lx0 + ci)),
                  pl.BlockSpec((None, l, CBLK), lambda bi, ci: (bi, 0, lg0 + ci)),
                  pl.BlockSpec((4, CBLK), lambda bi, ci: (0, ci)),
                  pl.BlockSpec((1, CBLK), lambda bi, ci: (0, ci)),
                  pl.BlockSpec((None, CBLK, 4 * CBLK), lambda bi, ci: (ci, 0, 0)),
                  pl.BlockSpec((None, 1, 4 * CBLK), lambda bi, ci: (ci, 0, 0)),
                  pl.BlockSpec((2, CBLK), lambda bi, ci: (0, ci)),
                  pl.BlockSpec((None, 2, CBLK), lambda bi, ci: (bi, 0, ci))],
        out_specs=[pl.BlockSpec((None, l, CBLK), lambda bi, ci: (bi, 0, ci)),
                   pl.BlockSpec((None, 2, CBLK), lambda bi, ci: (bi, 0, ci))],
        out_shape=[jax.ShapeDtypeStruct((b, l, LRU_WIDTH), BF16),
                   jax.ShapeDtypeStruct((b, 2, LRU_WIDTH), F32)],
        scratch_shapes=[pltpu.VMEM((l, CBLK), F32), pltpu.VMEM((l, CBLK), F32),
                        pltpu.VMEM((t, CBLK), F32), pltpu.VMEM((t, CBLK), F32)],
        compiler_params=_cparams(("arbitrary", "arbitrary")),
        name="lru",
    )(big, big, cw, cb, wg, bg, lam, h0)


def _ssdprep_kernel(x_ref, cw_ref, cb_ref, dt_ref, dtb_ref, o_ref, dto_ref, *, l, t):
    n_tiles = l // t

    def tile(i, carry):
        r0 = pl.multiple_of(i * t, t)
        y = _conv4_tile(x_ref, cw_ref, cb_ref, i, n_tiles, t, l)
        o_ref[pl.ds(r0, t), :] = (y * _sigmoid(y)).astype(BF16)
        return carry

    lax.fori_loop(0, n_tiles, tile, 0)

    @pl.when(pl.program_id(1) == 0)
    def _():
        v = dt_ref[...] + dtb_ref[...]
        dto_ref[...] = jnp.maximum(v, 0.0) + jnp.log1p(jnp.exp(-jnp.abs(v)))


def _ssdprep_call(big, dt, cw, cb, dtb):
    b, l, _ = big.shape
    t = 256
    nblk = SSD_CONV_DIM // CBLK
    x0 = OFF_XBC // CBLK
    kern = functools.partial(_ssdprep_kernel, l=l, t=t)
    return pl.pallas_call(
        kern,
        grid=(b, nblk),
        in_specs=[pl.BlockSpec((None, l, CBLK), lambda bi, ci: (bi, 0, x0 + ci)),
                  pl.BlockSpec((4, CBLK), lambda bi, ci: (0, ci)),
                  pl.BlockSpec((1, CBLK), lambda bi, ci: (0, ci)),
                  pl.BlockSpec((None, l, DT_W), lambda bi, ci: (bi, 0, 0)),
                  pl.BlockSpec((1, DT_W), lambda bi, ci: (0, 0))],
        out_specs=[pl.BlockSpec((None, l, CBLK), lambda bi, ci: (bi, 0, ci)),
                   pl.BlockSpec((None, l, DT_W), lambda bi, ci: (bi, 0, 0))],
        out_shape=[jax.ShapeDtypeStruct((b, l, SSD_CONV_DIM), BF16),
                   jax.ShapeDtypeStruct((b, l, DT_W), F32)],
        compiler_params=_cparams(("arbitrary", "arbitrary")),
        name="ssd_prep",
    )(big, cw, cb, dt, dtb)


def _ssdstate_kernel(xf_ref, xb_ref, dtf_ref, dtb_ref, a_ref, e_ref, s0_ref,
                     sef_ref, seb_ref, sfin_ref, sf_scr, sb_scr, xw_scr):
    c = pl.program_id(1)
    nc = pl.num_programs(1)

    @pl.when(c == 0)
    def _():
        sf_scr[...] = s0_ref[0]
        sb_scr[...] = s0_ref[1]

    a = a_ref[...]
    row = lax.broadcasted_iota(jnp.int32, (Q, Q), 0)
    col = lax.broadcasted_iota(jnp.int32, (Q, Q), 1)
    lane_lo = lax.broadcasted_iota(jnp.int32, (1, LANES), 1) < SSD_HEAD_DIM
    hp = SSD_HEADS // 2
    gp = hp // SSD_GROUPS

    for d in range(2):
        x_ref = (xf_ref, xb_ref)[d]
        dt = (dtf_ref, dtb_ref)[d][...]
        off = d * SSD_HEADS
        s_scr = (sf_scr, sb_scr)[d]
        se_ref = (sef_ref, seb_ref)[d]
        tri = jnp.where(col <= row, 1.0, 0.0) if d == 0 else jnp.where(col >= row, 1.0, 0.0)
        cs = _tri_dot(tri.astype(BF16), dt * a)
        tot = cs[Q - 1:Q, :] if d == 0 else cs[0:1, :]
        w = jnp.exp(tot - cs) * dt
        dec = jnp.exp(tot)
        se_ref[...] = s_scr[...].astype(BF16)
        w_hi = w.astype(BF16)
        w_lo = (w - w_hi.astype(F32)).astype(BF16)
        wexp = jnp.dot(jnp.concatenate([w_hi, w_lo], axis=1), e_ref[d], preferred_element_type=F32)
        xw_scr[...] = (x_ref[:, :SSD_INNER].astype(F32) * wexp).astype(BF16)
        for g in range(SSD_GROUPS):
            bg = x_ref[:, SSD_INNER + g * SSD_STATE:SSD_INNER + (g + 1) * SSD_STATE]
            sloc = lax.dot_general(bg, xw_scr[:, g * SSD_GROUP_W:(g + 1) * SSD_GROUP_W],
                                   (((0,), (0,)), ((), ())), preferred_element_type=F32)
            for jj in range(gp):
                j = g * gp + jj
                sl = slice(j * LANES, (j + 1) * LANES)
                s_scr[:, sl] = (s_scr[:, sl] * _pair_expand(dec, j, lane_lo, off)
                                + sloc[:, jj * LANES:(jj + 1) * LANES])

    @pl.when(c == nc - 1)
    def _():
        sfin_ref[0] = sf_scr[...]
        sfin_ref[1] = sb_scr[...]


def _head_expand_matrices():
    k = jnp.arange(2 * DT_W, dtype=jnp.int32)[:, None] % DT_W
    head = jnp.arange(SSD_INNER, dtype=jnp.int32)[None, :] // SSD_HEAD_DIM
    return jnp.stack([(k == head + d * SSD_HEADS) for d in range(2)]).astype(BF16)


def _ssdstate_call(xact, dtsp, a128, s0):
    b, l, _ = xact.shape
    nc = l // Q
    return pl.pallas_call(
        _ssdstate_kernel,
        grid=(b, nc),
        in_specs=[pl.BlockSpec((None, Q, SSD_CONV_DIM), lambda bi, ci: (bi, ci, 0)),
                  pl.BlockSpec((None, Q, SSD_CONV_DIM), lambda bi, ci: (bi, nc - 1 - ci, 0)),
                  pl.BlockSpec((None, Q, DT_W), lambda bi, ci: (bi, ci, 0)),
                  pl.BlockSpec((None, Q, DT_W), lambda bi, ci: (bi, nc - 1 - ci, 0)),
                  pl.BlockSpec((1, DT_W), lambda bi, ci: (0, 0)),
                  pl.BlockSpec((2, 2 * DT_W, SSD_INNER), lambda bi, ci: (0, 0, 0)),
                  pl.BlockSpec((None, 2, SSD_STATE, SSD_INNER), lambda bi, ci: (bi, 0, 0, 0))],
        out_specs=[pl.BlockSpec((None, None, SSD_STATE, SSD_INNER), lambda bi, ci: (bi, ci, 0, 0)),
                   pl.BlockSpec((None, None, SSD_STATE, SSD_INNER), lambda bi, ci: (bi, nc - 1 - ci, 0, 0)),
                   pl.BlockSpec((None, 2, SSD_STATE, SSD_INNER), lambda bi, ci: (bi, 0, 0, 0))],
        out_shape=[jax.ShapeDtypeStruct((b, nc, SSD_STATE, SSD_INNER), BF16),
                   jax.ShapeDtypeStruct((b, nc, SSD_STATE, SSD_INNER), BF16),
                   jax.ShapeDtypeStruct((b, 2, SSD_STATE, SSD_INNER), F32)],
        scratch_shapes=[pltpu.VMEM((SSD_STATE, SSD_INNER), F32), pltpu.VMEM((SSD_STATE, SSD_INNER), F32),
                        pltpu.VMEM((Q, SSD_INNER), BF16)],
        compiler_params=_cparams(("arbitrary", "arbitrary")),
        name="ssd_states",
    )(xact, xact, dtsp, dtsp, a128, _head_expand_matrices(), s0)


def _ssdy_kernel(x_ref, dt_ref, a_ref, dsk_ref, sef_ref, seb_ref, z_ref, nw_ref, o_ref, y_scr):
    nh = SSD_HEADS
    row = lax.broadcasted_iota(jnp.int32, (Q, Q), 0)
    col = lax.broadcasted_iota(jnp.int32, (Q, Q), 1)
    low = col <= row
    lane = lax.broadcasted_iota(jnp.int32, (1, LANES), 1)
    lane_lo = lane < SSD_HEAD_DIM
    dt = dt_ref[...]
    da = dt * a_ref[...]
    cs_f = _tri_dot(jnp.where(low, 1.0, 0.0).astype(BF16), da)
    cs_b = _tri_dot(jnp.where(col >= row, 1.0, 0.0).astype(BF16), da)
    p = jnp.where(lane < nh, cs_f, jnp.where(lane < 2 * nh, cs_b, dt))
    pt = p.T
    eye = col == row
    hp_per_g = nh // 2 // SSD_GROUPS

    for g in range(SSD_GROUPS):
        bg = x_ref[:, SSD_INNER + g * SSD_STATE:SSD_INNER + (g + 1) * SSD_STATE]
        cg = x_ref[:, SSD_INNER + (SSD_GROUPS + g) * SSD_STATE:SSD_INNER + (SSD_GROUPS + g + 1) * SSD_STATE]
        cb = lax.dot_general(cg, bg, (((1,), (1,)), ((), ())), preferred_element_type=F32)
        gsl = slice(g * SSD_GROUP_W, (g + 1) * SSD_GROUP_W)
        yoff_f = jnp.dot(cg, sef_ref[:, gsl], preferred_element_type=F32)
        yoff_b = jnp.dot(cg, seb_ref[:, gsl], preferred_element_type=F32)
        for jj in range(hp_per_g):
            j = g * hp_per_g + jj
            sl = slice(j * LANES, (j + 1) * LANES)
            xp = x_ref[:, sl]
            res, cfs, cbs = [], [], []
            for h in (2 * j, 2 * j + 1):
                cf = jnp.broadcast_to(p[:, h:h + 1], (Q, Q))
                cbk = jnp.broadcast_to(p[:, nh + h:nh + h + 1], (Q, Q))
                dtf_r = pt[2 * nh + h:2 * nh + h + 1, :]
                dtb_r = pt[3 * nh + h:3 * nh + h + 1, :]
                arg = jnp.where(low, cf - pt[h:h + 1, :], cbk - pt[nh + h:nh + h + 1, :])
                dts = jnp.where(eye, dtf_r + dtb_r, jnp.where(low, dtf_r, dtb_r))
                m = cb * (jnp.exp(arg) * dts) + jnp.where(eye, dsk_ref[h], 0.0)
                res.append(jnp.dot(m.astype(BF16), xp, preferred_element_type=F32))
                cfs.append(cf)
                cbs.append(cbk)
            sc_f = jnp.exp(jnp.where(lane_lo, cfs[0], cfs[1]))
            sc_b = jnp.exp(jnp.where(lane_lo, cbs[0], cbs[1]))
            y = (jnp.where(lane_lo, res[0], res[1])
                 + sc_f * yoff_f[:, jj * LANES:(jj + 1) * LANES]
                 + sc_b * yoff_b[:, jj * LANES:(jj + 1) * LANES])
            y_scr[:, sl] = y
        yg = y_scr[:, gsl]
        zg = z_ref[:, gsl].astype(F32)
        gated = yg * (zg * _sigmoid(zg))
        ms = jnp.mean(gated * gated, axis=-1, keepdims=True)
        o_ref[:, gsl] = (gated * lax.rsqrt(ms + EPS) * nw_ref[:, gsl]).astype(BF16)


def _ssdy_call(xact, dtsp, a128, dskip, sef, seb, big, nw):
    b, l, _ = xact.shape
    nc = l // Q
    return pl.pallas_call(
        _ssdy_kernel,
        grid=(b, nc),
        in_specs=[pl.BlockSpec((None, Q, SSD_CONV_DIM), lambda bi, ci: (bi, ci, 0)),
                  pl.BlockSpec((None, Q, DT_W), lambda bi, ci: (bi, ci, 0)),
                  pl.BlockSpec((1, DT_W), lambda bi, ci: (0, 0)),
                  pl.BlockSpec(memory_space=pltpu.SMEM),
                  pl.BlockSpec((None, None, SSD_STATE, SSD_INNER), lambda bi, ci: (bi, ci, 0, 0)),
                  pl.BlockSpec((None, None, SSD_STATE, SSD_INNER), lambda bi, ci: (bi, ci, 0, 0)),
                  pl.BlockSpec((None, Q, SSD_INNER), lambda bi, ci: (bi, ci, OFF_Z // SSD_INNER)),
                  pl.BlockSpec((1, SSD_INNER), lambda bi, ci: (0, 0))],
        out_specs=pl.BlockSpec((None, Q, SSD_INNER), lambda bi, ci: (bi, ci, 0)),
        out_shape=jax.ShapeDtypeStruct((b, l, SSD_INNER), BF16),
        scratch_shapes=[pltpu.VMEM((Q, SSD_INNER), F32)],
        compiler_params=_cparams(("arbitrary", "arbitrary")),
        name="ssd_y",
    )(xact, dtsp, a128, dskip, sef, seb, big, nw)


def _merge_kernel(ra_ref, gn_ref, gt_ref, x_ref, mod_ref, wl_ref, ws_ref, wo_ref, o_ref):
    ra = jnp.dot(ra_ref[...], wl_ref[...], preferred_element_type=F32)
    rb = jnp.dot(gn_ref[...], ws_ref[...], preferred_element_type=F32)
    ga = _sigmoid(gt_ref[:, :D_MODEL].astype(F32))
    gb = _sigmoid(gt_ref[:, D_MODEL:].astype(F32))
    m = (ga * ra + gb * rb).astype(BF16)
    o = jnp.dot(m, wo_ref[...], preferred_element_type=F32)
    o_ref[...] = x_ref[...] + mod_ref[2:3, :] * o


def _merge_call(ra, gn, big, x, mod, wl, ws, wo):
    b, l, d = x.shape
    tm = min(512, l)
    const = lambda bi, i: (0, 0)
    return pl.pallas_call(
        _merge_kernel,
        grid=(b, l // tm),
        in_specs=[pl.BlockSpec((None, tm, LRU_WIDTH), lambda bi, i: (bi, i, 0)),
                  pl.BlockSpec((None, tm, SSD_INNER), lambda bi, i: (bi, i, 0)),
                  pl.BlockSpec((None, tm, 2 * d), lambda bi, i: (bi, i, OFF_GT // (2 * D_MODEL))),
                  pl.BlockSpec((None, tm, d), lambda bi, i: (bi, i, 0)),
                  pl.BlockSpec((None, 6, d), lambda bi, i: (bi, 0, 0)),
                  pl.BlockSpec((LRU_WIDTH, d), const),
                  pl.BlockSpec((SSD_INNER, d), const),
                  pl.BlockSpec((d, d), const)],
        out_specs=pl.BlockSpec((None, tm, d), lambda bi, i: (bi, i, 0)),
        out_shape=jax.ShapeDtypeStruct((b, l, d), F32),
        compiler_params=_cparams(("arbitrary", "arbitrary")),
        name="merge",
    )(ra, gn, big, x, mod, wl, ws, wo)


def _ffn_kernel(*refs, tm, gw, halo, final_norm):
    if halo:
        (x_ref, xp_ref, xn_ref, nw_ref, mod_ref, wu_ref, cw_ref, cb_ref, wd_ref, fnw_ref,
         o_ref, h_scr, u_scr, v_scr, acc_scr) = refs
    else:
        (x_ref, nw_ref, mod_ref, wu_ref, cw_ref, cb_ref, wd_ref, fnw_ref,
         o_ref, h_scr, u_scr, v_scr, acc_scr) = refs
    i = pl.program_id(1)
    n_i = pl.num_programs(1)
    nw = nw_ref[...]
    shift = mod_ref[3:4, :]
    scale = mod_ref[4:5, :]
    ext = tm + 2 * gw if halo else tm
    top = gw if halo else 0
    pad = SUBLANES

    x = x_ref[...]
    h_scr[top:top + tm, :] = _norm_mod(x, nw, shift, scale).astype(BF16)
    if halo:
        hp = _norm_mod(xp_ref[...], nw, shift, scale)
        hn = _norm_mod(xn_ref[...], nw, shift, scale)
        h_scr[0:gw, :] = jnp.where(i > 0, hp, 0.0).astype(BF16)
        h_scr[gw + tm:ext, :] = jnp.where(i < n_i - 1, hn, 0.0).astype(BF16)
    for slot in range(2):
        u_scr[slot, 0:pad, :] = jnp.zeros((pad, FCH), F32)
        u_scr[slot, pad + ext:pad + ext + pad, :] = jnp.zeros((pad, FCH), F32)
    acc_scr[...] = jnp.zeros_like(acc_scr)

    def up(ci, slot):
        c0 = pl.multiple_of(ci * FCH, FCH)
        u_scr[slot, pad:pad + ext, :] = jnp.dot(h_scr[...], wu_ref[:, pl.ds(c0, FCH)],
                                                preferred_element_type=F32)
        v_scr[slot] = jnp.dot(h_scr[top:top + tm, :], wu_ref[:, pl.ds(FFN_DIM + c0, FCH)],
                              preferred_element_type=F32)

    def down(ci, slot):
        c0 = pl.multiple_of(ci * FCH, FCH)
        colid = lax.broadcasted_iota(jnp.int32, (ext, FCH), 0) & (gw - 1)
        u = u_scr[slot, pad:pad + ext, :]
        ul = jnp.where(colid >= 1, u_scr[slot, pad - 1:pad - 1 + ext, :], 0.0)
        ur = jnp.where(colid <= gw - 2, u_scr[slot, pad + 1:pad + 1 + ext, :], 0.0)
        cw = cw_ref[:, pl.ds(c0, FCH)]
        acc = cb_ref[:, pl.ds(c0, FCH)]
        for dr in ((-1, 0, 1) if halo else (0,)):
            o = top + dr * gw
            k = (dr + 1) * 3
            acc = (acc + cw[k:k + 1, :] * ul[o:o + tm] + cw[k + 1:k + 2, :] * u[o:o + tm]
                   + cw[k + 2:k + 3, :] * ur[o:o + tm])
        act = (_gelu_tanh(acc) * v_scr[slot]).astype(BF16)
        acc_scr[...] += jnp.dot(act, wd_ref[pl.ds(c0, FCH), :], preferred_element_type=F32)

    n_ch = FFN_DIM // FCH
    up(0, 0)

    def pair(pi, carry):
        c = 2 * pi
        up(c + 1, 1)
        down(c, 0)
        up(c + 2, 0)
        down(c + 1, 1)
        return carry

    lax.fori_loop(0, (n_ch - 1) // 2, pair, 0)
    down(n_ch - 1, 0)
    y = x + mod_ref[5:6, :] * acc_scr[...]
    if final_norm:
        ms = jnp.mean(y * y, axis=-1, keepdims=True)
        y = y * lax.rsqrt(ms + EPS) * fnw_ref[...]
    o_ref[...] = y


def _ffn_call(x, nw, mod, wu, cw, cb, wd, fnw, *, gw, halo, final_norm):
    b, l, d = x.shape
    tm = min(1024, l)
    assert tm % gw == 0 and gw & (gw - 1) == 0
    ext = tm + 2 * gw if halo else tm
    r = tm // gw
    ng = l // gw
    const = lambda bi, i: (0, 0)
    single = pl.Buffered(1)
    in_specs = [pl.BlockSpec((None, tm, d), lambda bi, i: (bi, i, 0))]
    args = [x]
    if halo:
        in_specs += [pl.BlockSpec((None, gw, d), lambda bi, i: (bi, jnp.maximum(i * r - 1, 0), 0)),
                     pl.BlockSpec((None, gw, d), lambda bi, i: (bi, jnp.minimum((i + 1) * r, ng - 1), 0))]
        args += [x, x]
    in_specs += [pl.BlockSpec((1, d), const),
                 pl.BlockSpec((None, 6, d), lambda bi, i: (bi, 0, 0)),
                 pl.BlockSpec((d, 2 * FFN_DIM), const, pipeline_mode=single),
                 pl.BlockSpec((9, FFN_DIM), const),
                 pl.BlockSpec((1, FFN_DIM), const),
                 pl.BlockSpec((FFN_DIM, d), const, pipeline_mode=single),
                 pl.BlockSpec((1, d), const)]
    args += [nw, mod, wu, cw, cb, wd, fnw]
    kern = functools.partial(_ffn_kernel, tm=tm, gw=gw, halo=halo, final_norm=final_norm)
    return pl.pallas_call(
        kern,
        grid=(b, l // tm),
        in_specs=in_specs,
        out_specs=pl.BlockSpec((None, tm, d), lambda bi, i: (bi, i, 0)),
        out_shape=jax.ShapeDtypeStruct((b, l, d), F32),
        scratch_shapes=[pltpu.VMEM((ext, d), BF16),
                        pltpu.VMEM((2, ext + 2 * SUBLANES, FCH), F32),
                        pltpu.VMEM((2, tm, FCH), F32),
                        pltpu.VMEM((tm, d), F32)],
        compiler_params=_cparams(("arbitrary", "arbitrary")),
        name="ffn",
    )(*args)


def _gate_weights(wa, wx):
    hpb = CBLK // LRU_HEAD_DIM
    nblk = LRU_WIDTH // CBLK
    eye = jnp.eye(hpb, dtype=F32)

    def blockdiag(w):
        w = w.reshape(nblk, hpb, LRU_HEAD_DIM, LRU_HEAD_DIM)
        bd = jnp.einsum('bhij,hk->bhikj', w, eye)
        return bd.reshape(nblk, CBLK, CBLK)

    parts = [blockdiag(wa[0]), blockdiag(wx[0]), blockdiag(wa[1]), blockdiag(wx[1])]
    return jnp.concatenate(parts, axis=-1).astype(BF16)


def _gate_biases(ba, bx):
    nblk = LRU_WIDTH // CBLK
    parts = [v.reshape(nblk, 1, CBLK) for v in (ba[0], bx[0], ba[1], bx[1])]
    return jnp.concatenate(parts, axis=-1)


def kernel(x, c, ctx, c_ctx, ada_w, ada_b, norm_mix_w, norm_ffn_w, w_in, lru_conv_w, lru_conv_b, lru_wa, lru_ba, lru_wx, lru_bx, lru_lambda, lru_proj, ssd_conv_w, ssd_conv_b, ssd_dt_bias, ssd_a_log, ssd_d, ssd_norm_w, ssd_proj, w_out, ffn_w_up, ffn_conv_w, ffn_conv_b, ffn_w_down, final_norm_w):
    b, l, d = x.shape
    lc = ctx.shape[1]
    depth = ada_w.shape[0]
    gw_x = 64
    assert d == D_MODEL and b + 1 <= 16 and l % Q == 0 and lc % Q == 0

    s = jnp.zeros((16, d), F32).at[:b].set(c).at[b].set(c_ctx)
    mods = _ada_call(s, ada_w, ada_b)

    o_lx, o_lg, o_z, o_xbc, o_dt, o_gt = 0, 1024, 2048, 4096, 7168, 7232
    h0 = jnp.zeros((b, 2, LRU_WIDTH), F32)
    s0 = jnp.zeros((b, 2, SSD_STATE, SSD_INNER), F32)
    fnw = final_norm_w.reshape(1, d)
    ctxf = ctx.reshape(1, b * lc, d)

    for li in range(depth):
        last = li == depth - 1
        modx = mods[li, :b].reshape(b, 6, d)
        modc = mods[li, b].reshape(1, 6, d)
        wl = w_in[li]
        w_big = jnp.concatenate([wl[:, o_z:o_xbc], wl[:, o_gt:], wl[:, o_lx:o_z], wl[:, o_xbc:o_dt]],
                                axis=1).astype(BF16)
        w_dt = jnp.concatenate([wl[:, o_dt:o_gt], wl[:, o_dt:o_gt]], axis=1).astype(BF16)
        nmw = norm_mix_w[li].reshape(1, d)
        nfw = norm_ffn_w[li].reshape(1, d)
        wg = _gate_weights(lru_wa[li], lru_wx[li])
        bgs = _gate_biases(lru_ba[li], lru_bx[li])
        lcw, lcb = lru_conv_w[li], lru_conv_b[li].reshape(1, -1)
        scw, scb = ssd_conv_w[li], ssd_conv_b[li].reshape(1, -1)
        dtb = jnp.tile(ssd_dt_bias[li].reshape(1, -1), (1, 2))
        a128 = jnp.tile(-jnp.exp(ssd_a_log[li].astype(F32)).reshape(1, -1), (1, 2))
        dsk = ssd_d[li].astype(F32)
        snw = ssd_norm_w[li].reshape(1, -1)
        wlp = lru_proj[li].astype(BF16)
        wsp = ssd_proj[li].astype(BF16)
        wo = w_out[li].astype(BF16)
        wu = ffn_w_up[li].astype(BF16)
        wd = ffn_w_down[li].astype(BF16)
        fcw = ffn_conv_w[li].reshape(9, FFN_DIM)
        fcb = ffn_conv_b[li].reshape(1, FFN_DIM)

        big_cf, dt_cf = _inproj_call(ctxf, nmw, modc, w_big, w_dt)
        big_c = big_cf.reshape(b, lc, BIG_W)
        dt_c = dt_cf.reshape(b, lc, DT_W)
        ra_c, hfin_c = _lru_call(big_c, lcw, lcb, wg, bgs, lru_lambda[li], h0)
        xact_c, dtsp_c = _ssdprep_call(big_c, dt_c, scw, scb, dtb)
        sef_c, seb_c, sfin_c = _ssdstate_call(xact_c, dtsp_c, a128, s0)

        big_x, dt_x = _inproj_call(x, nmw, modx, w_big, w_dt)
        ra_x, _ = _lru_call(big_x, lcw, lcb, wg, bgs, lru_lambda[li], hfin_c)
        xact_x, dtsp_x = _ssdprep_call(big_x, dt_x, scw, scb, dtb)
        sef_x, seb_x, _ = _ssdstate_call(xact_x, dtsp_x, a128, sfin_c)
        gn_x = _ssdy_call(xact_x, dtsp_x, a128, dsk, sef_x, seb_x, big_x, snw)
        x = _merge_call(ra_x, gn_x, big_x, x, modx, wlp, wsp, wo)
        x = _ffn_call(x, nfw, modx, wu, fcw, fcb, wd, fnw, gw=gw_x, halo=True, final_norm=last)

        if not last:
            gn_c = _ssdy_call(xact_c, dtsp_c, a128, dsk, sef_c, seb_c, big_c, snw)
            ctxf = _merge_call(ra_c.reshape(1, b * lc, LRU_WIDTH), gn_c.reshape(1, b * lc, SSD_INNER),
                               big_cf, ctxf, modc, wlp, wsp, wo)
            ctxf = _ffn_call(ctxf, nfw, modc, wu, fcw, fcb, wd, fnw, gw=lc, halo=False, final_norm=False)
    return x
```

```python
import functools

import jax
import jax.numpy as jnp
from jax import lax
from jax.experimental import pallas as pl
from jax.experimental.pallas import tpu as pltpu

F32 = jnp.float32
BF16 = jnp.bfloat16

EPS = 1e-6
LOG2E = 1.4426950408889634
D_MODEL = 1024
LRU_WIDTH = 1024
LRU_HEADS = 16
LRU_HEAD_DIM = 64
LRU_C = 8.0
SSD_INNER = 2048
SSD_HEADS = 32
SSD_HEAD_DIM = 64
SSD_GROUPS = 4
SSD_STATE = 128
SSD_CONV_DIM = SSD_INNER + 2 * SSD_GROUPS * SSD_STATE
SSD_GROUP_W = SSD_INNER // SSD_GROUPS
FFN_DIM = 2816

LANES = 128
SUBLANES = 8
VMEM_LIMIT_BYTES = 56 * 1024 * 1024

OFF_Z = 0
OFF_GT = OFF_Z + SSD_INNER
OFF_LX = OFF_GT + 2 * D_MODEL
OFF_LG = OFF_LX + LRU_WIDTH
OFF_XBC = OFF_LG + LRU_WIDTH
BIG_W = OFF_XBC + SSD_CONV_DIM
DT_W = 128

CBLK = 256
LRU_PITCH = 40
CONV_HALO = 16
Q = 128
FCH = 256
DOWN_SPLITS = (4, 8, 11)
assert Q == LANES and DOWN_SPLITS[-1] == FFN_DIM // FCH


def _cparams(sem):
    return pltpu.CompilerParams(dimension_semantics=sem, vmem_limit_bytes=VMEM_LIMIT_BYTES)


def _gelu_tanh(x):
    return 0.5 * x * (1.0 + jnp.tanh(0.7978845608028654 * (x + 0.044715 * (x * x * x))))


def _sigmoid(x):
    return 1.0 / (1.0 + jnp.exp(-x))


def _sigmoid_tanh(x):
    return 0.5 * jnp.tanh(0.5 * x) + 0.5


def _norm_mod(x, nw, shift, scale):
    ms = jnp.mean(x * x, axis=-1, keepdims=True)
    y = x * lax.rsqrt(ms + EPS) * nw
    return y * (1.0 + scale) + shift


def _pair_expand(v, j, lane_lo, off=0):
    return jnp.where(lane_lo, v[:, off + 2 * j:off + 2 * j + 1], v[:, off + 2 * j + 1:off + 2 * j + 2])


def _tri_dot(tri_bf16, v):
    hi = v.astype(BF16)
    r1 = v - hi.astype(F32)
    mid = r1.astype(BF16)
    lo = (r1 - mid.astype(F32)).astype(BF16)
    return (jnp.dot(tri_bf16, hi, preferred_element_type=F32)
            + jnp.dot(tri_bf16, mid, preferred_element_type=F32)
            + jnp.dot(tri_bf16, lo, preferred_element_type=F32))


def _ada_kernel(s_ref, w_ref, b_ref, o_ref):
    s = s_ref[...]
    s = s * _sigmoid(s)
    o_ref[...] = jnp.dot(s, w_ref[...], preferred_element_type=F32,
                         precision=lax.Precision.HIGHEST) + b_ref[...]


def _ada_call(s, ada_w, ada_b):
    depth, d, n = ada_w.shape
    tn = 1536
    rows = s.shape[0]
    return pl.pallas_call(
        _ada_kernel,
        grid=(depth, n // tn),
        in_specs=[pl.BlockSpec((rows, d), lambda l, j: (0, 0)),
                  pl.BlockSpec((None, d, tn), lambda l, j: (l, 0, j)),
                  pl.BlockSpec((None, 1, tn), lambda l, j: (l, 0, j))],
        out_specs=pl.BlockSpec((None, rows, tn), lambda l, j: (l, 0, j)),
        out_shape=jax.ShapeDtypeStruct((depth, rows, n), F32),
        compiler_params=_cparams(("arbitrary", "arbitrary")),
        name="ada",
    )(s, ada_w, ada_b.reshape(depth, 1, n))


def _inproj_kernel(x_ref, nw_ref, mod_ref, w_ref, wdt_ref, big_ref, dt_ref, h_scr):
    @pl.when(pl.program_id(2) == 0)
    def _():
        h = _norm_mod(x_ref[...], nw_ref[...], mod_ref[0:1, :], mod_ref[1:2, :]).astype(BF16)
        h_scr[...] = h
        dt_ref[...] = jnp.dot(h, wdt_ref[...], preferred_element_type=F32)

    big_ref[...] = jnp.dot(h_scr[...], w_ref[...], preferred_element_type=F32).astype(big_ref.dtype)


def _inproj_call(x, nw, mod, w, wdt):
    b, l, d = x.shape
    tm = min(2048, l)
    tn = 1536
    return pl.pallas_call(
        _inproj_kernel,
        grid=(b, l // tm, BIG_W // tn),
        in_specs=[pl.BlockSpec((None, tm, d), lambda bi, i, j: (bi, i, 0)),
                  pl.BlockSpec((1, d), lambda bi, i, j: (0, 0)),
                  pl.BlockSpec((None, 6, d), lambda bi, i, j: (bi, 0, 0)),
                  pl.BlockSpec((d, tn), lambda bi, i, j: (0, j)),
                  pl.BlockSpec((d, DT_W), lambda bi, i, j: (0, 0))],
        out_specs=[pl.BlockSpec((None, tm, tn), lambda bi, i, j: (bi, i, j)),
                   pl.BlockSpec((None, tm, DT_W), lambda bi, i, j: (bi, i, 0))],
        out_shape=[jax.ShapeDtypeStruct((b, l, BIG_W), BF16),
                   jax.ShapeDtypeStruct((b, l, DT_W), F32)],
        scratch_shapes=[pltpu.VMEM((tm, d), BF16)],
        compiler_params=_cparams(("arbitrary", "arbitrary", "arbitrary")),
        name="in_proj",
    )(x, nw, mod, w, wdt)


def _conv4_vpu_tile(x_ref, cw_ref, cb_ref, i, n_tiles, t, l):
    hr = CONV_HALO
    r0 = pl.multiple_of(i * t, t)
    main = x_ref[pl.ds(r0, t), :].astype(F32)
    prev = x_ref[pl.ds(pl.multiple_of(jnp.maximum(r0 - hr, 0), hr), hr), :].astype(F32)
    nxt = x_ref[pl.ds(pl.multiple_of(jnp.minimum(r0 + t, l - hr), hr), hr), :].astype(F32)
    prev = jnp.where(i > 0, prev, 0.0)
    nxt = jnp.where(i < n_tiles - 1, nxt, 0.0)
    win = jnp.concatenate([prev, main, nxt], axis=0)
    acc = cb_ref[...] + cw_ref[0:1, :] * win[hr - 2:hr - 2 + t]
    for k in range(1, 4):
        acc = acc + cw_ref[k:k + 1, :] * win[hr - 2 + k:hr - 2 + k + t]
    return acc


def _conv_shift_matrix(t, permuted):
    q = jnp.arange(t, dtype=jnp.int32)
    tq = (q % SUBLANES) * (t // SUBLANES) + q // SUBLANES if permuted else q
    src = tq[None, :, None] + jnp.arange(4, dtype=jnp.int32)[:, None, None] - 2
    return (src == jnp.arange(t, dtype=jnp.int32)[None, None, :]).reshape(4 * t, t).astype(BF16)


def _conv4_shift(x_ref, s_ref, sh_scr, i, slot, t):
    r0 = pl.multiple_of(i * t, t)
    sh_scr[slot] = jnp.dot(s_ref[...], x_ref[pl.ds(r0, t), :], preferred_element_type=F32)


def _conv4_tile(x_ref, sh_scr, slot, cw_ref, cb_ref, i, n_tiles, t, l, permuted):
    hr = CONV_HALO
    r0 = pl.multiple_of(i * t, t)
    acc = cb_ref[...] + cw_ref[0:1, :] * sh_scr[slot, 0:t, :]
    for k in range(1, 4):
        acc = acc + cw_ref[k:k + 1, :] * sh_scr[slot, k * t:(k + 1) * t, :]
    prev = x_ref[pl.ds(pl.multiple_of(jnp.maximum(r0 - hr, 0), hr), hr), :].astype(F32)
    nxt = x_ref[pl.ds(pl.multiple_of(jnp.minimum(r0 + t, l - hr), hr), hr), :].astype(F32)
    pm2 = jnp.where(i > 0, prev[hr - 2:hr - 1], 0.0)
    pm1 = jnp.where(i > 0, prev[hr - 1:hr], 0.0)
    np0 = jnp.where(i < n_tiles - 1, nxt[0:1], 0.0)
    row8 = lax.broadcasted_iota(jnp.int32, (SUBLANES, acc.shape[1]), 0)
    f0 = cw_ref[0:1, :] * pm2 + cw_ref[1:2, :] * pm1
    f1 = cw_ref[0:1, :] * pm1
    fl = cw_ref[3:4, :] * np0
    sl = SUBLANES
    if permuted:
        head = [acc[0:sl] + jnp.where(row8 == 0, f0, 0.0), acc[sl:2 * sl] + jnp.where(row8 == 0, f1, 0.0)]
    else:
        head = [acc[0:sl] + jnp.where(row8 == 0, f0, jnp.where(row8 == 1, f1, 0.0)), acc[sl:2 * sl]]
    tail = acc[t - sl:t] + jnp.where(row8 == sl - 1, fl, 0.0)
    return jnp.concatenate(head + [acc[2 * sl:t - sl], tail], axis=0)


def _pipelined_tiles(n_tiles, shift, body, carry):
    shift(0, 0)
    if n_tiles == 1:
        return body(0, 0, carry)
    assert n_tiles % 2 == 0

    def pair(ip, c):
        i = 2 * ip
        shift(i + 1, 1)
        c = body(i, 0, c)
        shift(jnp.minimum(i + 2, n_tiles - 1), 0)
        return body(i + 1, 1, c)

    return lax.fori_loop(0, n_tiles // 2, pair, carry)


def _lru_kernel(lx_ref, lg_ref, s_ref, cw_ref, cb_ref, wg_ref, bg_ref, lam_ref, h0_ref,
                o_ref, hfin_ref, u_scr, y_scr, perm_scr, sh_scr, *, l, t):
    n_tiles = l // t
    c = CBLK
    seg = t // SUBLANES
    nslab = c // LANES
    lam = lam_ref[...]
    c8 = LRU_C * (jnp.minimum(lam, 0.0) - jnp.log1p(jnp.exp(-jnp.abs(lam))))
    row8 = lax.broadcasted_iota(jnp.int32, (SUBLANES, c), 0)

    def unpermute(val):
        for j in range(seg):
            for k in range(nslab):
                perm_scr[k, pl.ds(j, SUBLANES, stride=LRU_PITCH), :] = val[j * SUBLANES:(j + 1) * SUBLANES, k * LANES:(k + 1) * LANES]
        segs = [jnp.concatenate([perm_scr[k, s * LRU_PITCH:s * LRU_PITCH + seg, :] for k in range(nslab)], axis=1)
                for s in range(SUBLANES)]
        return jnp.concatenate(segs, axis=0)

    def gates(u, d):
        g = jnp.dot(u.astype(BF16), wg_ref[:, d * 2 * c:(d + 1) * 2 * c],
                    preferred_element_type=F32) + bg_ref[:, d * 2 * c:(d + 1) * 2 * c]
        r = _sigmoid_tanh(g[:, :c])
        ig = _sigmoid_tanh(g[:, c:])
        log_a = c8[d:d + 1, :] * r
        a = jnp.exp(log_a)
        s = -jnp.tanh(log_a) * (a * a + 1.0)
        bb = jnp.where(s > 0.0, s * lax.rsqrt(s), 0.0) * (ig * u)
        return a, bb

    def scan(a, bb, hc, fwd):
        order = list(range(seg)) if fwd else list(range(seg - 1, -1, -1))
        hl, pr = [None] * seg, [None] * seg
        prev = None
        for j in order:
            aj = a[j * SUBLANES:(j + 1) * SUBLANES]
            bj = bb[j * SUBLANES:(j + 1) * SUBLANES]
            if prev is None:
                hl[j], pr[j] = bj, aj
            else:
                hl[j], pr[j] = aj * hl[prev] + bj, aj * pr[prev]
            prev = j
        av, bv = pr[prev], hl[prev]
        for k in (1, 2, 4):
            sh = k if fwd else SUBLANES - k
            m = (row8 >= k) if fwd else (row8 < SUBLANES - k)
            a_s = pltpu.roll(av, sh, 0)
            b_s = pltpu.roll(bv, sh, 0)
            bv = jnp.where(m, av * b_s + bv, bv)
            av = jnp.where(m, av * a_s, av)
        e = bv + av * hc
        if fwd:
            cin = jnp.where(row8 == 0, hc, pltpu.roll(e, 1, 0))
            hout = e[SUBLANES - 1:SUBLANES, :]
        else:
            cin = jnp.where(row8 == SUBLANES - 1, hc, pltpu.roll(e, SUBLANES - 1, 0))
            hout = e[0:1, :]
        h = jnp.concatenate([hl[j] + pr[j] * cin for j in range(seg)], axis=0)
        return h, hout

    def fwd_tile(i, slot, hprev):
        r0 = pl.multiple_of(i * t, t)
        up = _conv4_tile(lx_ref, sh_scr, slot, cw_ref, cb_ref, i, n_tiles, t, l, True)
        u_scr[pl.ds(r0, t), :] = up
        a, bb = gates(up, 0)
        h, hout = scan(a, bb, hprev, True)
        y_scr[pl.ds(r0, t), :] = h
        return hout

    hf = _pipelined_tiles(n_tiles, lambda i, slot: _conv4_shift(lx_ref, s_ref, sh_scr, i, slot, t),
                          fwd_tile, h0_ref[0:1, :])
    hfin_ref[0:1, :] = hf

    def bwd_tile(ii, hnext):
        i = n_tiles - 1 - ii
        r0 = pl.multiple_of(i * t, t)
        a, bb = gates(u_scr[pl.ds(r0, t), :], 1)
        h, hout = scan(a, bb, hnext, False)
        y = unpermute(y_scr[pl.ds(r0, t), :] + h)
        o_ref[pl.ds(r0, t), :] = (y * _gelu_tanh(lg_ref[pl.ds(r0, t), :].astype(F32))).astype(BF16)
        return hout

    hb = lax.fori_loop(0, n_tiles, bwd_tile, h0_ref[1:2, :])
    hfin_ref[1:2, :] = hb


def _lru_call(big, cw, cb, wg, bg, lam, h0):
    b, l, _ = big.shape
    t = 256
    nblk = LRU_WIDTH // CBLK
    lx0 = OFF_LX // CBLK
    lg0 = OFF_LG // CBLK
    kern = functools.partial(_lru_kernel, l=l, t=t)
    return pl.pallas_call(
        kern,
        grid=(b, nblk),
        in_specs=[pl.BlockSpec((None, l, CBLK), lambda bi, ci: (bi, 0, lx0 + ci)),
                  pl.BlockSpec((None, l, CBLK), lambda bi, ci: (bi, 0, lg0 + ci)),
                  pl.BlockSpec((4 * t, t), lambda bi, ci: (0, 0)),
                  pl.BlockSpec((4, CBLK), lambda bi, ci: (0, ci)),
                  pl.BlockSpec((1, CBLK), lambda bi, ci: (0, ci)),
                  pl.BlockSpec((None, CBLK, 4 * CBLK), lambda bi, ci: (ci, 0, 0)),
                  pl.BlockSpec((None, 1, 4 * CBLK), lambda bi, ci: (ci, 0, 0)),
                  pl.BlockSpec((2, CBLK), lambda bi, ci: (0, ci)),
                  pl.BlockSpec((None, 2, CBLK), lambda bi, ci: (bi, 0, ci))],
        out_specs=[pl.BlockSpec((None, l, CBLK), lambda bi, ci: (bi, 0, ci)),
                   pl.BlockSpec((None, 2, CBLK), lambda bi, ci: (bi, 0, ci))],
        out_shape=[jax.ShapeDtypeStruct((b, l, LRU_WIDTH), BF16),
                   jax.ShapeDtypeStruct((b, 2, LRU_WIDTH), F32)],
        scratch_shapes=[pltpu.VMEM((l, CBLK), F32), pltpu.VMEM((l, CBLK), F32),
                        pltpu.VMEM((CBLK // LANES, SUBLANES * LRU_PITCH, LANES), F32),
                        pltpu.VMEM((2, 4 * t, CBLK), F32)],
        compiler_params=_cparams(("arbitrary", "arbitrary")),
        name="lru",
    )(big, big, _conv_shift_matrix(t, True), cw, cb, wg, bg, lam, h0)


def _ssdprep_kernel(x_ref, cw_ref, cb_ref, dt_ref, dtb_ref, o_ref, dto_ref, *, l, t):
    n_tiles = l // t

    def tile(i, carry):
        r0 = pl.multiple_of(i * t, t)
        y = _conv4_vpu_tile(x_ref, cw_ref, cb_ref, i, n_tiles, t, l)
        o_ref[pl.ds(r0, t), :] = (y * _sigmoid_tanh(y)).astype(BF16)
        return carry

    lax.fori_loop(0, n_tiles, tile, 0)

    @pl.when(pl.program_id(1) == 0)
    def _():
        v = dt_ref[...] + dtb_ref[...]
        dto_ref[...] = jnp.maximum(v, 0.0) + jnp.log1p(jnp.exp(-jnp.abs(v)))


def _ssdprep_call(big, dt, cw, cb, dtb):
    b, l, _ = big.shape
    t = 256
    nblk = SSD_CONV_DIM // CBLK
    x0 = OFF_XBC // CBLK
    kern = functools.partial(_ssdprep_kernel, l=l, t=t)
    return pl.pallas_call(
        kern,
        grid=(b, nblk),
        in_specs=[pl.BlockSpec((None, l, CBLK), lambda bi, ci: (bi, 0, x0 + ci)),
                  pl.BlockSpec((4, CBLK), lambda bi, ci: (0, ci)),
                  pl.BlockSpec((1, CBLK), lambda bi, ci: (0, ci)),
                  pl.BlockSpec((None, l, DT_W), lambda bi, ci: (bi, 0, 0)),
                  pl.BlockSpec((1, DT_W), lambda bi, ci: (0, 0))],
        out_specs=[pl.BlockSpec((None, l, CBLK), lambda bi, ci: (bi, 0, ci)),
                   pl.BlockSpec((None, l, DT_W), lambda bi, ci: (bi, 0, 0))],
        out_shape=[jax.ShapeDtypeStruct((b, l, SSD_CONV_DIM), BF16),
                   jax.ShapeDtypeStruct((b, l, DT_W), F32)],
        compiler_params=_cparams(("arbitrary", "arbitrary")),
        name="ssd_prep",
    )(big, cw, cb, dt, dtb)


def _ssdstate_kernel(xf_ref, xb_ref, dtf_ref, dtb_ref, a_ref, e_ref, s0_ref,
                     sef_ref, seb_ref, sfin_ref, sf_scr, sb_scr, xw_scr):
    c = pl.program_id(1)
    nc = pl.num_programs(1)

    @pl.when(c == 0)
    def _():
        sf_scr[...] = s0_ref[0]
        sb_scr[...] = s0_ref[1]

    a = a_ref[...]
    row = lax.broadcasted_iota(jnp.int32, (Q, Q), 0)
    col = lax.broadcasted_iota(jnp.int32, (Q, Q), 1)
    lane_lo = lax.broadcasted_iota(jnp.int32, (1, LANES), 1) < SSD_HEAD_DIM
    hp = SSD_HEADS // 2
    gp = hp // SSD_GROUPS

    for d in range(2):
        x_ref = (xf_ref, xb_ref)[d]
        dt = (dtf_ref, dtb_ref)[d][...]
        off = d * SSD_HEADS
        s_scr = (sf_scr, sb_scr)[d]
        se_ref = (sef_ref, seb_ref)[d]
        tri = jnp.where(col <= row, 1.0, 0.0) if d == 0 else jnp.where(col >= row, 1.0, 0.0)
        cs = _tri_dot(tri.astype(BF16), dt * a)
        tot = cs[Q - 1:Q, :] if d == 0 else cs[0:1, :]
        w = jnp.exp(tot - cs) * dt
        dec = jnp.exp(tot)
        se_ref[...] = s_scr[...].astype(BF16)
        wexp = jnp.dot(_split_hi_lo(w), e_ref[d], preferred_element_type=F32)
        xw_scr[...] = (x_ref[:, :SSD_INNER].astype(F32) * wexp).astype(BF16)
        for g in range(SSD_GROUPS):
            bg = x_ref[:, SSD_INNER + g * SSD_STATE:SSD_INNER + (g + 1) * SSD_STATE]
            sloc = lax.dot_general(bg, xw_scr[:, g * SSD_GROUP_W:(g + 1) * SSD_GROUP_W],
                                   (((0,), (0,)), ((), ())), preferred_element_type=F32)
            for jj in range(gp):
                j = g * gp + jj
                sl = slice(j * LANES, (j + 1) * LANES)
                s_scr[:, sl] = (s_scr[:, sl] * _pair_expand(dec, j, lane_lo, off)
                                + sloc[:, jj * LANES:(jj + 1) * LANES])

    @pl.when(c == nc - 1)
    def _():
        sfin_ref[0] = sf_scr[...]
        sfin_ref[1] = sb_scr[...]


def _head_expand_matrices(offsets):
    k = jnp.arange(2 * DT_W, dtype=jnp.int32)[:, None] % DT_W
    head = jnp.arange(SSD_INNER, dtype=jnp.int32)[None, :] // SSD_HEAD_DIM
    return jnp.stack([(k == head + off) for off in offsets]).astype(BF16)


def _split_hi_lo(v):
    hi = v.astype(BF16)
    lo = (v - hi.astype(F32)).astype(BF16)
    return jnp.concatenate([hi, lo], axis=1)


def _ssdstate_call(xact, dtsp, a128, s0):
    b, l, _ = xact.shape
    nc = l // Q
    return pl.pallas_call(
        _ssdstate_kernel,
        grid=(b, nc),
        in_specs=[pl.BlockSpec((None, Q, SSD_CONV_DIM), lambda bi, ci: (bi, ci, 0)),
                  pl.BlockSpec((None, Q, SSD_CONV_DIM), lambda bi, ci: (bi, nc - 1 - ci, 0)),
                  pl.BlockSpec((None, Q, DT_W), lambda bi, ci: (bi, ci, 0)),
                  pl.BlockSpec((None, Q, DT_W), lambda bi, ci: (bi, nc - 1 - ci, 0)),
                  pl.BlockSpec((1, DT_W), lambda bi, ci: (0, 0)),
                  pl.BlockSpec((2, 2 * DT_W, SSD_INNER), lambda bi, ci: (0, 0, 0)),
                  pl.BlockSpec((None, 2, SSD_STATE, SSD_INNER), lambda bi, ci: (bi, 0, 0, 0))],
        out_specs=[pl.BlockSpec((None, None, SSD_STATE, SSD_INNER), lambda bi, ci: (bi, ci, 0, 0)),
                   pl.BlockSpec((None, None, SSD_STATE, SSD_INNER), lambda bi, ci: (bi, nc - 1 - ci, 0, 0)),
                   pl.BlockSpec((None, 2, SSD_STATE, SSD_INNER), lambda bi, ci: (bi, 0, 0, 0))],
        out_shape=[jax.ShapeDtypeStruct((b, nc, SSD_STATE, SSD_INNER), BF16),
                   jax.ShapeDtypeStruct((b, nc, SSD_STATE, SSD_INNER), BF16),
                   jax.ShapeDtypeStruct((b, 2, SSD_STATE, SSD_INNER), F32)],
        scratch_shapes=[pltpu.VMEM((SSD_STATE, SSD_INNER), F32), pltpu.VMEM((SSD_STATE, SSD_INNER), F32),
                        pltpu.VMEM((Q, SSD_INNER), BF16)],
        compiler_params=_cparams(("arbitrary", "arbitrary")),
        name="ssd_states",
    )(xact, xact, dtsp, dtsp, a128, _head_expand_matrices((0, SSD_HEADS)), s0)


def _ssdy_kernel(x_ref, dt_ref, a_ref, dsk_ref, e_ref, sef_ref, seb_ref, z_ref, nw_ref, o_ref, y_scr):
    nh = SSD_HEADS
    row = lax.broadcasted_iota(jnp.int32, (Q, Q), 0)
    col = lax.broadcasted_iota(jnp.int32, (Q, Q), 1)
    low = col <= row
    lane = lax.broadcasted_iota(jnp.int32, (1, LANES), 1)
    lane_lo = lane < SSD_HEAD_DIM
    dt = dt_ref[...]
    da = dt * a_ref[...]
    cs_f = _tri_dot(jnp.where(low, 1.0, 0.0).astype(BF16), da)
    cs_b = _tri_dot(jnp.where(col >= row, 1.0, 0.0).astype(BF16), da)
    p2 = jnp.where(lane < nh, cs_f, jnp.where(lane < 2 * nh, cs_b, jnp.log(dt))) * LOG2E
    pt2 = p2.T
    sc = _split_hi_lo(jnp.exp2(p2))
    sc_f = jnp.dot(sc, e_ref[0], preferred_element_type=F32)
    sc_b = jnp.dot(sc, e_ref[1], preferred_element_type=F32)
    hp_per_g = nh // 2 // SSD_GROUPS

    for g in range(SSD_GROUPS):
        bg = x_ref[:, SSD_INNER + g * SSD_STATE:SSD_INNER + (g + 1) * SSD_STATE]
        cg = x_ref[:, SSD_INNER + (SSD_GROUPS + g) * SSD_STATE:SSD_INNER + (SSD_GROUPS + g + 1) * SSD_STATE]
        cb = lax.dot_general(cg, bg, (((1,), (1,)), ((), ())), preferred_element_type=F32)
        gsl = slice(g * SSD_GROUP_W, (g + 1) * SSD_GROUP_W)
        yoff_f = jnp.dot(cg, sef_ref[:, gsl], preferred_element_type=F32)
        yoff_b = jnp.dot(cg, seb_ref[:, gsl], preferred_element_type=F32)
        dcb = jnp.sum(cg.astype(F32) * bg.astype(F32), axis=1, keepdims=True)
        dself = jnp.dot(_split_hi_lo(dcb * dt + dsk_ref[...]), e_ref[2, :, gsl], preferred_element_type=F32)
        for jj in range(hp_per_g):
            j = g * hp_per_g + jj
            sl = slice(j * LANES, (j + 1) * LANES)
            xp = x_ref[:, sl]
            res = []
            for h in (2 * j, 2 * j + 1):
                r_f = pt2[h:h + 1, :] - pt2[2 * nh + h:2 * nh + h + 1, :]
                r_b = pt2[nh + h:nh + h + 1, :] - pt2[3 * nh + h:3 * nh + h + 1, :]
                arg = jnp.where(low, p2[:, h:h + 1] - r_f, p2[:, nh + h:nh + h + 1] - r_b)
                m = (cb * jnp.exp2(arg)).astype(BF16)
                res.append(jnp.dot(m, xp, preferred_element_type=F32))
            y_scr[:, sl] = jnp.where(lane_lo, res[0], res[1])
        yg = (y_scr[:, gsl] + sc_f[:, gsl] * yoff_f + sc_b[:, gsl] * yoff_b
              + dself * x_ref[:, gsl].astype(F32))
        zg = z_ref[:, gsl].astype(F32)
        gated = yg * (zg * _sigmoid_tanh(zg))
        ms = jnp.mean(gated * gated, axis=-1, keepdims=True)
        o_ref[:, gsl] = (gated * lax.rsqrt(ms + EPS) * nw_ref[:, gsl]).astype(BF16)


def _ssdy_call(xact, dtsp, a128, dskip, sef, seb, big, nw):
    b, l, _ = xact.shape
    nc = l // Q
    return pl.pallas_call(
        _ssdy_kernel,
        grid=(b, nc),
        in_specs=[pl.BlockSpec((None, Q, SSD_CONV_DIM), lambda bi, ci: (bi, ci, 0)),
                  pl.BlockSpec((None, Q, DT_W), lambda bi, ci: (bi, ci, 0)),
                  pl.BlockSpec((1, DT_W), lambda bi, ci: (0, 0)),
                  pl.BlockSpec((1, DT_W), lambda bi, ci: (0, 0)),
                  pl.BlockSpec((3, 2 * DT_W, SSD_INNER), lambda bi, ci: (0, 0, 0)),
                  pl.BlockSpec((None, None, SSD_STATE, SSD_INNER), lambda bi, ci: (bi, ci, 0, 0)),
                  pl.BlockSpec((None, None, SSD_STATE, SSD_INNER), lambda bi, ci: (bi, ci, 0, 0)),
                  pl.BlockSpec((None, Q, SSD_INNER), lambda bi, ci: (bi, ci, OFF_Z // SSD_INNER)),
                  pl.BlockSpec((1, SSD_INNER), lambda bi, ci: (0, 0))],
        out_specs=pl.BlockSpec((None, Q, SSD_INNER), lambda bi, ci: (bi, ci, 0)),
        out_shape=jax.ShapeDtypeStruct((b, l, SSD_INNER), BF16),
        scratch_shapes=[pltpu.VMEM((Q, SSD_INNER), F32)],
        compiler_params=_cparams(("arbitrary", "arbitrary")),
        name="ssd_y",
    )(xact, dtsp, a128, dskip, _head_expand_matrices((0, SSD_HEADS, 3 * SSD_HEADS)), sef, seb, big, nw)


def _merge_kernel(ra_ref, gn_ref, gt_ref, x_ref, mod_ref, wl_ref, ws_ref, wo_ref, o_ref):
    ra = jnp.dot(ra_ref[...], wl_ref[...], preferred_element_type=F32)
    rb = jnp.dot(gn_ref[...], ws_ref[...], preferred_element_type=F32)
    ga = _sigmoid(gt_ref[:, :D_MODEL].astype(F32))
    gb = _sigmoid(gt_ref[:, D_MODEL:].astype(F32))
    m = (ga * ra + gb * rb).astype(BF16)
    o = jnp.dot(m, wo_ref[...], preferred_element_type=F32)
    o_ref[...] = x_ref[...] + mod_ref[2:3, :] * o


def _merge_call(ra, gn, big, x, mod, wl, ws, wo):
    b, l, d = x.shape
    tm = min(512, l)
    const = lambda bi, i: (0, 0)
    return pl.pallas_call(
        _merge_kernel,
        grid=(b, l // tm),
        in_specs=[pl.BlockSpec((None, tm, LRU_WIDTH), lambda bi, i: (bi, i, 0)),
                  pl.BlockSpec((None, tm, SSD_INNER), lambda bi, i: (bi, i, 0)),
                  pl.BlockSpec((None, tm, 2 * d), lambda bi, i: (bi, i, OFF_GT // (2 * D_MODEL))),
                  pl.BlockSpec((None, tm, d), lambda bi, i: (bi, i, 0)),
                  pl.BlockSpec((None, 6, d), lambda bi, i: (bi, 0, 0)),
                  pl.BlockSpec((LRU_WIDTH, d), const),
                  pl.BlockSpec((SSD_INNER, d), const),
                  pl.BlockSpec((d, d), const)],
        out_specs=pl.BlockSpec((None, tm, d), lambda bi, i: (bi, i, 0)),
        out_shape=jax.ShapeDtypeStruct((b, l, d), F32),
        compiler_params=_cparams(("arbitrary", "arbitrary")),
        name="merge",
    )(ra, gn, big, x, mod, wl, ws, wo)


def _ffn_kernel(*refs, tm, gw, halo, final_norm):
    if halo:
        (x_ref, xp_ref, xn_ref, nw_ref, mod_ref, wu_ref, cw_ref, cb_ref, wd_ref, fnw_ref,
         o_ref, h_scr, u_scr, v_scr, act_scr) = refs
    else:
        (x_ref, nw_ref, mod_ref, wu_ref, cw_ref, cb_ref, wd_ref, fnw_ref,
         o_ref, h_scr, u_scr, v_scr, act_scr) = refs
    i = pl.program_id(1)
    n_i = pl.num_programs(1)
    nw = nw_ref[...]
    shift = mod_ref[3:4, :]
    scale = mod_ref[4:5, :]
    ext = tm + 2 * gw if halo else tm
    top = gw if halo else 0
    pad = SUBLANES

    x = x_ref[...]
    h_scr[top:top + tm, :] = _norm_mod(x, nw, shift, scale).astype(BF16)
    if halo:
        hp = _norm_mod(xp_ref[...], nw, shift, scale)
        hn = _norm_mod(xn_ref[...], nw, shift, scale)
        h_scr[0:gw, :] = jnp.where(i > 0, hp, 0.0).astype(BF16)
        h_scr[gw + tm:ext, :] = jnp.where(i < n_i - 1, hn, 0.0).astype(BF16)
    for slot in range(2):
        u_scr[slot, 0:pad, :] = jnp.zeros((pad, FCH), F32)
        u_scr[slot, pad + ext:pad + ext + pad, :] = jnp.zeros((pad, FCH), F32)

    def up(ci, slot):
        c0 = pl.multiple_of(ci * FCH, FCH)
        u_scr[slot, pad:pad + ext, :] = jnp.dot(h_scr[...], wu_ref[:, pl.ds(c0, FCH)],
                                                preferred_element_type=F32)
        v_scr[slot] = jnp.dot(h_scr[top:top + tm, :], wu_ref[:, pl.ds(FFN_DIM + c0, FCH)],
                              preferred_element_type=F32)

    def down(ci, slot):
        c0 = pl.multiple_of(ci * FCH, FCH)
        colid = lax.broadcasted_iota(jnp.int32, (ext, FCH), 0) & (gw - 1)
        u = u_scr[slot, pad:pad + ext, :]
        ul = jnp.where(colid >= 1, u_scr[slot, pad - 1:pad - 1 + ext, :], 0.0)
        ur = jnp.where(colid <= gw - 2, u_scr[slot, pad + 1:pad + 1 + ext, :], 0.0)
        cw = cw_ref[:, pl.ds(c0, FCH)]
        acc = cb_ref[:, pl.ds(c0, FCH)]
        for dr in ((-1, 0, 1) if halo else (0,)):
            o = top + dr * gw
            k = (dr + 1) * 3
            acc = (acc + cw[k:k + 1, :] * ul[o:o + tm] + cw[k + 1:k + 2, :] * u[o:o + tm]
                   + cw[k + 2:k + 3, :] * ur[o:o + tm])
        act_scr[:, pl.ds(c0, FCH)] = (_gelu_tanh(acc) * v_scr[slot]).astype(BF16)

    n_ch = FFN_DIM // FCH
    up(0, 0)
    acc = None
    k0 = 0
    for c in range(n_ch):
        if c + 1 < n_ch:
            up(c + 1, (c + 1) % 2)
        down(c, c % 2)
        if (c + 1) in DOWN_SPLITS:
            k1 = (c + 1) * FCH
            part = jnp.dot(act_scr[:, k0:k1], wd_ref[k0:k1, :], preferred_element_type=F32)
            acc = part if acc is None else acc + part
            k0 = k1
    y = x + mod_ref[5:6, :] * acc
    if final_norm:
        ms = jnp.mean(y * y, axis=-1, keepdims=True)
        y = y * lax.rsqrt(ms + EPS) * fnw_ref[...]
    o_ref[...] = y


def _ffn_call(x, nw, mod, wu, cw, cb, wd, fnw, *, gw, halo, final_norm):
    b, l, d = x.shape
    tm = min(1024, l)
    assert tm % gw == 0 and gw & (gw - 1) == 0
    ext = tm + 2 * gw if halo else tm
    r = tm // gw
    ng = l // gw
    const = lambda bi, i: (0, 0)
    single = pl.Buffered(1)
    in_specs = [pl.BlockSpec((None, tm, d), lambda bi, i: (bi, i, 0))]
    args = [x]
    if halo:
        in_specs += [pl.BlockSpec((None, gw, d), lambda bi, i: (bi, jnp.maximum(i * r - 1, 0), 0)),
                     pl.BlockSpec((None, gw, d), lambda bi, i: (bi, jnp.minimum((i + 1) * r, ng - 1), 0))]
        args += [x, x]
    in_specs += [pl.BlockSpec((1, d), const),
                 pl.BlockSpec((None, 6, d), lambda bi, i: (bi, 0, 0)),
                 pl.BlockSpec((d, 2 * FFN_DIM), const, pipeline_mode=single),
                 pl.BlockSpec((9, FFN_DIM), const),
                 pl.BlockSpec((1, FFN_DIM), const),
                 pl.BlockSpec((FFN_DIM, d), const, pipeline_mode=single),
                 pl.BlockSpec((1, d), const)]
    args += [nw, mod, wu, cw, cb, wd, fnw]
    kern = functools.partial(_ffn_kernel, tm=tm, gw=gw, halo=halo, final_norm=final_norm)
    return pl.pallas_call(
        kern,
        grid=(b, l // tm),
        in_specs=in_specs,
        out_specs=pl.BlockSpec((None, tm, d), lambda bi, i: (bi, i, 0)),
        out_shape=jax.ShapeDtypeStruct((b, l, d), F32),
        scratch_shapes=[pltpu.VMEM((ext, d), BF16),
                        pltpu.VMEM((2, ext + 2 * SUBLANES, FCH), F32),
                        pltpu.VMEM((2, tm, FCH), F32),
                        pltpu.VMEM((tm, FFN_DIM), BF16)],
        compiler_params=_cparams(("arbitrary", "arbitrary")),
        name="ffn",
    )(*args)


def _gate_weights(wa, wx):
    hpb = CBLK // LRU_HEAD_DIM
    nblk = LRU_WIDTH // CBLK
    eye = jnp.eye(hpb, dtype=F32)

    def blockdiag(w):
        w = w.reshape(nblk, hpb, LRU_HEAD_DIM, LRU_HEAD_DIM)
        bd = jnp.einsum('bhij,hk->bhikj', w, eye)
        return bd.reshape(nblk, CBLK, CBLK)

    parts = [blockdiag(wa[0]), blockdiag(wx[0]), blockdiag(wa[1]), blockdiag(wx[1])]
    return jnp.concatenate(parts, axis=-1).astype(BF16)


def _gate_biases(ba, bx):
    nblk = LRU_WIDTH // CBLK
    parts = [v.reshape(nblk, 1, CBLK) for v in (ba[0], bx[0], ba[1], bx[1])]
    return jnp.concatenate(parts, axis=-1)


def kernel(x, c, ctx, c_ctx, ada_w, ada_b, norm_mix_w, norm_ffn_w, w_in, lru_conv_w, lru_conv_b, lru_wa, lru_ba, lru_wx, lru_bx, lru_lambda, lru_proj, ssd_conv_w, ssd_conv_b, ssd_dt_bias, ssd_a_log, ssd_d, ssd_norm_w, ssd_proj, w_out, ffn_w_up, ffn_conv_w, ffn_conv_b, ffn_w_down, final_norm_w):
    b, l, d = x.shape
    lc = ctx.shape[1]
    depth = ada_w.shape[0]
    gw_x = 64
    assert d == D_MODEL and b + 1 <= 16 and l % Q == 0 and lc % Q == 0

    s = jnp.zeros((16, d), F32).at[:b].set(c).at[b].set(c_ctx)
    mods = _ada_call(s, ada_w, ada_b)

    o_lx, o_lg, o_z, o_xbc, o_dt, o_gt = 0, 1024, 2048, 4096, 7168, 7232
    h0 = jnp.zeros((b, 2, LRU_WIDTH), F32)
    s0 = jnp.zeros((b, 2, SSD_STATE, SSD_INNER), F32)
    fnw = final_norm_w.reshape(1, d)
    ctxf = ctx.reshape(1, b * lc, d)

    for li in range(depth):
        last = li == depth - 1
        modx = mods[li, :b].reshape(b, 6, d)
        modc = mods[li, b].reshape(1, 6, d)
        wl = w_in[li]
        w_big = jnp.concatenate([wl[:, o_z:o_xbc], wl[:, o_gt:], wl[:, o_lx:o_z], wl[:, o_xbc:o_dt]],
                                axis=1).astype(BF16)
        w_dt = jnp.concatenate([wl[:, o_dt:o_gt], wl[:, o_dt:o_gt]], axis=1).astype(BF16)
        nmw = norm_mix_w[li].reshape(1, d)
        nfw = norm_ffn_w[li].reshape(1, d)
        wg = _gate_weights(lru_wa[li], lru_wx[li])
        bgs = _gate_biases(lru_ba[li], lru_bx[li])
        lcw, lcb = lru_conv_w[li], lru_conv_b[li].reshape(1, -1)
        scw, scb = ssd_conv_w[li], ssd_conv_b[li].reshape(1, -1)
        dtb = jnp.tile(ssd_dt_bias[li].reshape(1, -1), (1, 2))
        a128 = jnp.tile(-jnp.exp(ssd_a_log[li].astype(F32)).reshape(1, -1), (1, 2))
        dsk = jnp.concatenate([jnp.zeros((1, DT_W - SSD_HEADS), F32), ssd_d[li].reshape(1, -1)], axis=1)
        snw = ssd_norm_w[li].reshape(1, -1)
        wlp = lru_proj[li].astype(BF16)
        wsp = ssd_proj[li].astype(BF16)
        wo = w_out[li].astype(BF16)
        wu = ffn_w_up[li].astype(BF16)
        wd = ffn_w_down[li].astype(BF16)
        fcw = ffn_conv_w[li].reshape(9, FFN_DIM)
        fcb = ffn_conv_b[li].reshape(1, FFN_DIM)

        big_cf, dt_cf = _inproj_call(ctxf, nmw, modc, w_big, w_dt)
        big_c = big_cf.reshape(b, lc, BIG_W)
        dt_c = dt_cf.reshape(b, lc, DT_W)
        ra_c, hfin_c = _lru_call(big_c, lcw, lcb, wg, bgs, lru_lambda[li], h0)
        xact_c, dtsp_c = _ssdprep_call(big_c, dt_c, scw, scb, dtb)
        sef_c, seb_c, sfin_c = _ssdstate_call(xact_c, dtsp_c, a128, s0)

        big_x, dt_x = _inproj_call(x, nmw, modx, w_big, w_dt)
        ra_x, _ = _lru_call(big_x, lcw, lcb, wg, bgs, lru_lambda[li], hfin_c)
        xact_x, dtsp_x = _ssdprep_call(big_x, dt_x, scw, scb, dtb)
        sef_x, seb_x, _ = _ssdstate_call(xact_x, dtsp_x, a128, sfin_c)
        gn_x = _ssdy_call(xact_x, dtsp_x, a128, dsk, sef_x, seb_x, big_x, snw)
        x = _merge_call(ra_x, gn_x, big_x, x, modx, wlp, wsp, wo)
        x = _ffn_call(x, nfw, modx, wu, fcw, fcb, wd, fnw, gw=gw_x, halo=True, final_norm=last)

        if not last:
            gn_c = _ssdy_call(xact_c, dtsp_c, a128, dsk, sef_c, seb_c, big_c, snw)
            ctxf = _merge_call(ra_c.reshape(1, b * lc, LRU_WIDTH), gn_c.reshape(1, b * lc, SSD_INNER),
                               big_cf, ctxf, modc, wlp, wsp, wo)
            ctxf = _ffn_call(ctxf, nfw, modc, wu, fcw, fcb, wd, fnw, gw=lc, halo=False, final_norm=False)
    return x
```

```python
import functools

import jax
import jax.numpy as jnp
from jax import lax
from jax.experimental import pallas as pl
from jax.experimental.pallas import tpu as pltpu

F32 = jnp.float32
BF16 = jnp.bfloat16

EPS = 1e-6
LOG2E = 1.4426950408889634
D_MODEL = 1024
LRU_WIDTH = 1024
LRU_HEADS = 16
LRU_HEAD_DIM = 64
LRU_C = 8.0
SSD_INNER = 2048
SSD_HEADS = 32
SSD_HEAD_DIM = 64
SSD_GROUPS = 4
SSD_STATE = 128
SSD_CONV_DIM = SSD_INNER + 2 * SSD_GROUPS * SSD_STATE
SSD_GROUP_W = SSD_INNER // SSD_GROUPS
FFN_DIM = 2816

LANES = 128
SUBLANES = 8
VMEM_LIMIT_BYTES = 56 * 1024 * 1024

OFF_Z = 0
OFF_GT = OFF_Z + SSD_INNER
OFF_LX = OFF_GT + 2 * D_MODEL
OFF_LG = OFF_LX + LRU_WIDTH
OFF_XBC = OFF_LG + LRU_WIDTH
BIG_W = OFF_XBC + SSD_CONV_DIM
DT_W = 128

CBLK = 256
LRU_CBLK = 512
LRU_PITCH = 40
CONV_HALO = 16
Q = 128
CPS = 2
FCH = 256
DOWN_SPLITS = (4, 8, 11)
assert Q == LANES and DOWN_SPLITS[-1] == FFN_DIM // FCH


def _cparams(sem):
    return pltpu.CompilerParams(dimension_semantics=sem, vmem_limit_bytes=VMEM_LIMIT_BYTES)


def _gelu_tanh(x):
    return 0.5 * x * (1.0 + jnp.tanh(0.7978845608028654 * (x + 0.044715 * (x * x * x))))


def _sigmoid(x):
    return 1.0 / (1.0 + jnp.exp(-x))


def _sigmoid_tanh(x):
    return 0.5 * jnp.tanh(0.5 * x) + 0.5


def _norm_mod(x, nw, shift, scale):
    ms = jnp.mean(x * x, axis=-1, keepdims=True)
    y = x * lax.rsqrt(ms + EPS) * nw
    return y * (1.0 + scale) + shift


def _pair_expand(v, j, lane_lo, off=0):
    return jnp.where(lane_lo, v[:, off + 2 * j:off + 2 * j + 1], v[:, off + 2 * j + 1:off + 2 * j + 2])


def _tri_dot(tri_bf16, v):
    hi = v.astype(BF16)
    r1 = v - hi.astype(F32)
    mid = r1.astype(BF16)
    lo = (r1 - mid.astype(F32)).astype(BF16)
    return (jnp.dot(tri_bf16, hi, preferred_element_type=F32)
            + jnp.dot(tri_bf16, mid, preferred_element_type=F32)
            + jnp.dot(tri_bf16, lo, preferred_element_type=F32))


def _ada_kernel(s_ref, w_ref, b_ref, o_ref):
    s = s_ref[...]
    s = s * _sigmoid(s)
    o_ref[...] = jnp.dot(s, w_ref[...], preferred_element_type=F32,
                         precision=lax.Precision.HIGHEST) + b_ref[...]


def _ada_call(s, ada_w, ada_b):
    depth, d, n = ada_w.shape
    tn = 1536
    rows = s.shape[0]
    return pl.pallas_call(
        _ada_kernel,
        grid=(depth, n // tn),
        in_specs=[pl.BlockSpec((rows, d), lambda l, j: (0, 0)),
                  pl.BlockSpec((None, d, tn), lambda l, j: (l, 0, j)),
                  pl.BlockSpec((None, 1, tn), lambda l, j: (l, 0, j))],
        out_specs=pl.BlockSpec((None, rows, tn), lambda l, j: (l, 0, j)),
        out_shape=jax.ShapeDtypeStruct((depth, rows, n), F32),
        compiler_params=_cparams(("arbitrary", "arbitrary")),
        name="ada",
    )(s, ada_w, ada_b.reshape(depth, 1, n))


def _inproj_kernel(x_ref, nw_ref, mod_ref, w_ref, wdt_ref, big_ref, dt_ref, h_scr):
    @pl.when(pl.program_id(2) == 0)
    def _():
        h = _norm_mod(x_ref[...], nw_ref[...], mod_ref[0:1, :], mod_ref[1:2, :]).astype(BF16)
        h_scr[...] = h
        dt_ref[...] = jnp.dot(h, wdt_ref[...], preferred_element_type=F32)

    big_ref[...] = jnp.dot(h_scr[...], w_ref[...], preferred_element_type=F32).astype(big_ref.dtype)


def _inproj_call(x, nw, mod, w, wdt):
    b, l, d = x.shape
    tm = min(2048, l)
    tn = 1536
    return pl.pallas_call(
        _inproj_kernel,
        grid=(b, l // tm, BIG_W // tn),
        in_specs=[pl.BlockSpec((None, tm, d), lambda bi, i, j: (bi, i, 0)),
                  pl.BlockSpec((1, d), lambda bi, i, j: (0, 0)),
                  pl.BlockSpec((None, 6, d), lambda bi, i, j: (bi, 0, 0)),
                  pl.BlockSpec((d, tn), lambda bi, i, j: (0, j)),
                  pl.BlockSpec((d, DT_W), lambda bi, i, j: (0, 0))],
        out_specs=[pl.BlockSpec((None, tm, tn), lambda bi, i, j: (bi, i, j)),
                   pl.BlockSpec((None, tm, DT_W), lambda bi, i, j: (bi, i, 0))],
        out_shape=[jax.ShapeDtypeStruct((b, l, BIG_W), BF16),
                   jax.ShapeDtypeStruct((b, l, DT_W), F32)],
        scratch_shapes=[pltpu.VMEM((tm, d), BF16)],
        compiler_params=_cparams(("arbitrary", "arbitrary", "arbitrary")),
        name="in_proj",
    )(x, nw, mod, w, wdt)


def _conv4_vpu_tile(x_ref, cw_ref, cb_ref, i, n_tiles, t, l):
    hr = CONV_HALO
    r0 = pl.multiple_of(i * t, t)
    main = x_ref[pl.ds(r0, t), :].astype(F32)
    prev = x_ref[pl.ds(pl.multiple_of(jnp.maximum(r0 - hr, 0), hr), hr), :].astype(F32)
    nxt = x_ref[pl.ds(pl.multiple_of(jnp.minimum(r0 + t, l - hr), hr), hr), :].astype(F32)
    prev = jnp.where(i > 0, prev, 0.0)
    nxt = jnp.where(i < n_tiles - 1, nxt, 0.0)
    win = jnp.concatenate([prev, main, nxt], axis=0)
    n = t + 2 * hr
    acc = cb_ref[...] + cw_ref[2:3, :] * main
    for k in (0, 1, 3):
        acc = acc + cw_ref[k:k + 1, :] * pltpu.roll(win, (2 - k) % n, 0)[hr:hr + t]
    return acc


def _conv_shift_matrix(t, permuted):
    q = jnp.arange(t, dtype=jnp.int32)
    tq = (q % SUBLANES) * (t // SUBLANES) + q // SUBLANES if permuted else q
    src = tq[None, :, None] + jnp.arange(4, dtype=jnp.int32)[:, None, None] - 2
    return (src == jnp.arange(t, dtype=jnp.int32)[None, None, :]).reshape(4 * t, t).astype(BF16)


def _conv4_shift(x_ref, s_ref, sh_scr, i, slot, t):
    r0 = pl.multiple_of(i * t, t)
    sh_scr[slot] = jnp.dot(s_ref[...], x_ref[pl.ds(r0, t), :], preferred_element_type=F32)


def _conv4_tile(x_ref, sh_scr, slot, cw_ref, cb_ref, i, n_tiles, t, l, permuted):
    hr = CONV_HALO
    r0 = pl.multiple_of(i * t, t)
    acc = cb_ref[...] + cw_ref[0:1, :] * sh_scr[slot, 0:t, :]
    for k in range(1, 4):
        acc = acc + cw_ref[k:k + 1, :] * sh_scr[slot, k * t:(k + 1) * t, :]
    prev = x_ref[pl.ds(pl.multiple_of(jnp.maximum(r0 - hr, 0), hr), hr), :].astype(F32)
    nxt = x_ref[pl.ds(pl.multiple_of(jnp.minimum(r0 + t, l - hr), hr), hr), :].astype(F32)
    pm2 = jnp.where(i > 0, prev[hr - 2:hr - 1], 0.0)
    pm1 = jnp.where(i > 0, prev[hr - 1:hr], 0.0)
    np0 = jnp.where(i < n_tiles - 1, nxt[0:1], 0.0)
    row8 = lax.broadcasted_iota(jnp.int32, (SUBLANES, acc.shape[1]), 0)
    f0 = cw_ref[0:1, :] * pm2 + cw_ref[1:2, :] * pm1
    f1 = cw_ref[0:1, :] * pm1
    fl = cw_ref[3:4, :] * np0
    sl = SUBLANES
    if permuted:
        head = [acc[0:sl] + jnp.where(row8 == 0, f0, 0.0), acc[sl:2 * sl] + jnp.where(row8 == 0, f1, 0.0)]
    else:
        head = [acc[0:sl] + jnp.where(row8 == 0, f0, jnp.where(row8 == 1, f1, 0.0)), acc[sl:2 * sl]]
    tail = acc[t - sl:t] + jnp.where(row8 == sl - 1, fl, 0.0)
    return jnp.concatenate(head + [acc[2 * sl:t - sl], tail], axis=0)


def _pipelined_tiles(n_tiles, shift, body, carry):
    shift(0, 0)
    if n_tiles == 1:
        return body(0, 0, carry)
    assert n_tiles % 2 == 0

    def pair(ip, c):
        i = 2 * ip
        shift(i + 1, 1)
        c = body(i, 0, c)
        shift(jnp.minimum(i + 2, n_tiles - 1), 0)
        return body(i + 1, 1, c)

    return lax.fori_loop(0, n_tiles // 2, pair, carry)


def _lru_kernel(lx_ref, s_ref, cw_ref, cb_ref, wg_ref, bg_ref, lam_ref, h0_ref,
                o_ref, hfin_ref, u_scr, y_scr, perm_scr, sh_scr, *, l, t):
    n_tiles = l // t
    c = LRU_CBLK
    gw = CBLK
    seg = t // SUBLANES
    nslab = c // LANES
    lam = lam_ref[...]
    c8 = LRU_C * (jnp.minimum(lam, 0.0) - jnp.log1p(jnp.exp(-jnp.abs(lam))))
    row8 = lax.broadcasted_iota(jnp.int32, (SUBLANES, c), 0)

    def unpermute(val):
        for j in range(seg):
            for k in range(nslab):
                perm_scr[k, pl.ds(j, SUBLANES, stride=LRU_PITCH), :] = val[j * SUBLANES:(j + 1) * SUBLANES, k * LANES:(k + 1) * LANES]
        segs = [jnp.concatenate([perm_scr[k, s * LRU_PITCH:s * LRU_PITCH + seg, :] for k in range(nslab)], axis=1)
                for s in range(SUBLANES)]
        return jnp.concatenate(segs, axis=0)

    def gates(u, d):
        ub = u.astype(BF16)
        gs = [jnp.dot(ub[:, q * gw:(q + 1) * gw], wg_ref[q, :, d * 2 * gw:(d + 1) * 2 * gw],
                      preferred_element_type=F32) + bg_ref[q, :, d * 2 * gw:(d + 1) * 2 * gw] for q in range(c // gw)]
        r = _sigmoid_tanh(jnp.concatenate([g[:, :gw] for g in gs], axis=1))
        ig = _sigmoid_tanh(jnp.concatenate([g[:, gw:] for g in gs], axis=1))
        log_a = c8[d:d + 1, :] * r
        a = jnp.exp(log_a)
        s = -jnp.tanh(log_a) * (a * a + 1.0)
        bb = jnp.where(s > 0.0, s * lax.rsqrt(s), 0.0) * (ig * u)
        return a, bb

    def scan(a, bb, hc, fwd):
        order = list(range(seg)) if fwd else list(range(seg - 1, -1, -1))
        hl, pr = [None] * seg, [None] * seg
        prev = None
        for j in order:
            aj = a[j * SUBLANES:(j + 1) * SUBLANES]
            bj = bb[j * SUBLANES:(j + 1) * SUBLANES]
            if prev is None:
                hl[j], pr[j] = bj, aj
            else:
                hl[j], pr[j] = aj * hl[prev] + bj, aj * pr[prev]
            prev = j
        av, bv = pr[prev], hl[prev]
        for k in (1, 2, 4):
            sh = k if fwd else SUBLANES - k
            m = (row8 >= k) if fwd else (row8 < SUBLANES - k)
            a_s = pltpu.roll(av, sh, 0)
            b_s = pltpu.roll(bv, sh, 0)
            bv = jnp.where(m, av * b_s + bv, bv)
            av = jnp.where(m, av * a_s, av)
        e = bv + av * hc
        if fwd:
            cin = jnp.where(row8 == 0, hc, pltpu.roll(e, 1, 0))
            hout = e[SUBLANES - 1:SUBLANES, :]
        else:
            cin = jnp.where(row8 == SUBLANES - 1, hc, pltpu.roll(e, SUBLANES - 1, 0))
            hout = e[0:1, :]
        h = jnp.concatenate([hl[j] + pr[j] * cin for j in range(seg)], axis=0)
        return h, hout

    def fwd_tile(i, slot, hprev):
        r0 = pl.multiple_of(i * t, t)
        up = _conv4_tile(lx_ref, sh_scr, slot, cw_ref, cb_ref, i, n_tiles, t, l, True)
        u_scr[pl.ds(r0, t), :] = up
        a, bb = gates(up, 0)
        h, hout = scan(a, bb, hprev, True)
        y_scr[pl.ds(r0, t), :] = h
        return hout

    hf = _pipelined_tiles(n_tiles, lambda i, slot: _conv4_shift(lx_ref, s_ref, sh_scr, i, slot, t),
                          fwd_tile, h0_ref[0:1, :])
    hfin_ref[0:1, :] = hf

    def bwd_tile(ii, hnext):
        i = n_tiles - 1 - ii
        r0 = pl.multiple_of(i * t, t)
        a, bb = gates(u_scr[pl.ds(r0, t), :], 1)
        h, hout = scan(a, bb, hnext, False)
        o_ref[pl.ds(r0, t), :] = unpermute(y_scr[pl.ds(r0, t), :] + h).astype(BF16)
        return hout

    hb = lax.fori_loop(0, n_tiles, bwd_tile, h0_ref[1:2, :])
    hfin_ref[1:2, :] = hb


def _lru_call(big, cw, cb, wg, bg, lam, h0):
    b, l, _ = big.shape
    t = 256
    c = LRU_CBLK
    nblk = LRU_WIDTH // c
    gpb = c // CBLK
    lx0 = OFF_LX // c
    kern = functools.partial(_lru_kernel, l=l, t=t)
    return pl.pallas_call(
        kern,
        grid=(b, nblk),
        in_specs=[pl.BlockSpec((None, l, c), lambda bi, ci: (bi, 0, lx0 + ci)),
                  pl.BlockSpec((4 * t, t), lambda bi, ci: (0, 0)),
                  pl.BlockSpec((4, c), lambda bi, ci: (0, ci)),
                  pl.BlockSpec((1, c), lambda bi, ci: (0, ci)),
                  pl.BlockSpec((gpb, CBLK, 4 * CBLK), lambda bi, ci: (ci, 0, 0)),
                  pl.BlockSpec((gpb, 1, 4 * CBLK), lambda bi, ci: (ci, 0, 0)),
                  pl.BlockSpec((2, c), lambda bi, ci: (0, ci)),
                  pl.BlockSpec((None, 2, c), lambda bi, ci: (bi, 0, ci))],
        out_specs=[pl.BlockSpec((None, l, c), lambda bi, ci: (bi, 0, ci)),
                   pl.BlockSpec((None, 2, c), lambda bi, ci: (bi, 0, ci))],
        out_shape=[jax.ShapeDtypeStruct((b, l, LRU_WIDTH), BF16),
                   jax.ShapeDtypeStruct((b, 2, LRU_WIDTH), F32)],
        scratch_shapes=[pltpu.VMEM((l, c), F32), pltpu.VMEM((l, c), F32),
                        pltpu.VMEM((c // LANES, SUBLANES * LRU_PITCH, LANES), F32),
                        pltpu.VMEM((2, 4 * t, c), F32)],
        compiler_params=_cparams(("arbitrary", "arbitrary")),
        name="lru",
    )(big, _conv_shift_matrix(t, True), cw, cb, wg, bg, lam, h0)


def _ssdprep_kernel(x_ref, cw_ref, cb_ref, dt_ref, dtb_ref, o_ref, dto_ref, *, l, t):
    n_tiles = l // t

    def tile(i, carry):
        r0 = pl.multiple_of(i * t, t)
        y = _conv4_vpu_tile(x_ref, cw_ref, cb_ref, i, n_tiles, t, l)
        o_ref[pl.ds(r0, t), :] = (y * _sigmoid_tanh(y)).astype(BF16)
        return carry

    lax.fori_loop(0, n_tiles, tile, 0)

    @pl.when(pl.program_id(1) == 0)
    def _():
        v = dt_ref[...] + dtb_ref[...]
        dto_ref[...] = jnp.maximum(v, 0.0) + jnp.log1p(jnp.exp(-jnp.abs(v)))


def _ssdprep_call(big, dt, cw, cb, dtb):
    b, l, _ = big.shape
    t = 256
    nblk = SSD_CONV_DIM // CBLK
    x0 = OFF_XBC // CBLK
    kern = functools.partial(_ssdprep_kernel, l=l, t=t)
    return pl.pallas_call(
        kern,
        grid=(b, nblk),
        in_specs=[pl.BlockSpec((None, l, CBLK), lambda bi, ci: (bi, 0, x0 + ci)),
                  pl.BlockSpec((4, CBLK), lambda bi, ci: (0, ci)),
                  pl.BlockSpec((1, CBLK), lambda bi, ci: (0, ci)),
                  pl.BlockSpec((None, l, DT_W), lambda bi, ci: (bi, 0, 0)),
                  pl.BlockSpec((1, DT_W), lambda bi, ci: (0, 0))],
        out_specs=[pl.BlockSpec((None, l, CBLK), lambda bi, ci: (bi, 0, ci)),
                   pl.BlockSpec((None, l, DT_W), lambda bi, ci: (bi, 0, 0))],
        out_shape=[jax.ShapeDtypeStruct((b, l, SSD_CONV_DIM), BF16),
                   jax.ShapeDtypeStruct((b, l, DT_W), F32)],
        compiler_params=_cparams(("arbitrary", "arbitrary")),
        name="ssd_prep",
    )(big, cw, cb, dt, dtb)


def _ssdstate_kernel(xf_ref, xb_ref, dtf_ref, dtb_ref, a_ref, e_ref, s0_ref,
                     sef_ref, seb_ref, sfin_ref, sf_scr, sb_scr, xw_scr):
    c = pl.program_id(1)
    nc = pl.num_programs(1)

    @pl.when(c == 0)
    def _():
        sf_scr[...] = s0_ref[0]
        sb_scr[...] = s0_ref[1]

    a = a_ref[...]
    row = lax.broadcasted_iota(jnp.int32, (Q, Q), 0)
    col = lax.broadcasted_iota(jnp.int32, (Q, Q), 1)
    lane_lo = lax.broadcasted_iota(jnp.int32, (1, LANES), 1) < SSD_HEAD_DIM
    hp = SSD_HEADS // 2
    gp = hp // SSD_GROUPS

    for d in range(2):
        x_ref = (xf_ref, xb_ref)[d]
        dt_ref = (dtf_ref, dtb_ref)[d]
        off = d * SSD_HEADS
        s_scr = (sf_scr, sb_scr)[d]
        se_ref = (sef_ref, seb_ref)[d]
        tri = (jnp.where(col <= row, 1.0, 0.0) if d == 0 else jnp.where(col >= row, 1.0, 0.0)).astype(BF16)
        for kk in range(CPS):
            sub = kk if d == 0 else CPS - 1 - kk
            rows = slice(sub * Q, (sub + 1) * Q)
            dt = dt_ref[rows, :]
            cs = _tri_dot(tri, dt * a)
            tot = cs[Q - 1:Q, :] if d == 0 else cs[0:1, :]
            w = jnp.exp(tot - cs) * dt
            dec = jnp.exp(tot)
            wexp = jnp.dot(_split_hi_lo(w), e_ref[d], preferred_element_type=F32)
            slot = d * CPS + kk
            xw_scr[slot] = (x_ref[rows, :SSD_INNER].astype(F32) * wexp).astype(BF16)
            se_ref[sub] = s_scr[...].astype(BF16)
            for g in range(SSD_GROUPS):
                bg = x_ref[rows, SSD_INNER + g * SSD_STATE:SSD_INNER + (g + 1) * SSD_STATE]
                sloc = lax.dot_general(bg, xw_scr[slot, :, g * SSD_GROUP_W:(g + 1) * SSD_GROUP_W],
                                       (((0,), (0,)), ((), ())), preferred_element_type=F32)
                for jj in range(gp):
                    j = g * gp + jj
                    sl = slice(j * LANES, (j + 1) * LANES)
                    s_scr[:, sl] = (s_scr[:, sl] * _pair_expand(dec, j, lane_lo, off)
                                    + sloc[:, jj * LANES:(jj + 1) * LANES])

    @pl.when(c == nc - 1)
    def _():
        sfin_ref[0] = sf_scr[...]
        sfin_ref[1] = sb_scr[...]


def _head_expand_matrices(offsets):
    k = jnp.arange(2 * DT_W, dtype=jnp.int32)[:, None] % DT_W
    head = jnp.arange(SSD_INNER, dtype=jnp.int32)[None, :] // SSD_HEAD_DIM
    return jnp.stack([(k == head + off) for off in offsets]).astype(BF16)


def _split_hi_lo(v):
    hi = v.astype(BF16)
    lo = (v - hi.astype(F32)).astype(BF16)
    return jnp.concatenate([hi, lo], axis=1)


def _ssdstate_call(xact, dtsp, a128, s0):
    b, l, _ = xact.shape
    nc = l // Q
    ns = nc // CPS
    qs = CPS * Q
    return pl.pallas_call(
        _ssdstate_kernel,
        grid=(b, ns),
        in_specs=[pl.BlockSpec((None, qs, SSD_CONV_DIM), lambda bi, ci: (bi, ci, 0)),
                  pl.BlockSpec((None, qs, SSD_CONV_DIM), lambda bi, ci: (bi, ns - 1 - ci, 0)),
                  pl.BlockSpec((None, qs, DT_W), lambda bi, ci: (bi, ci, 0)),
                  pl.BlockSpec((None, qs, DT_W), lambda bi, ci: (bi, ns - 1 - ci, 0)),
                  pl.BlockSpec((1, DT_W), lambda bi, ci: (0, 0)),
                  pl.BlockSpec((2, 2 * DT_W, SSD_INNER), lambda bi, ci: (0, 0, 0)),
                  pl.BlockSpec((None, 2, SSD_STATE, SSD_INNER), lambda bi, ci: (bi, 0, 0, 0))],
        out_specs=[pl.BlockSpec((None, CPS, SSD_STATE, SSD_INNER), lambda bi, ci: (bi, ci, 0, 0)),
                   pl.BlockSpec((None, CPS, SSD_STATE, SSD_INNER), lambda bi, ci: (bi, ns - 1 - ci, 0, 0)),
                   pl.BlockSpec((None, 2, SSD_STATE, SSD_INNER), lambda bi, ci: (bi, 0, 0, 0))],
        out_shape=[jax.ShapeDtypeStruct((b, nc, SSD_STATE, SSD_INNER), BF16),
                   jax.ShapeDtypeStruct((b, nc, SSD_STATE, SSD_INNER), BF16),
                   jax.ShapeDtypeStruct((b, 2, SSD_STATE, SSD_INNER), F32)],
        scratch_shapes=[pltpu.VMEM((SSD_STATE, SSD_INNER), F32), pltpu.VMEM((SSD_STATE, SSD_INNER), F32),
                        pltpu.VMEM((2 * CPS, Q, SSD_INNER), BF16)],
        compiler_params=_cparams(("arbitrary", "arbitrary")),
        name="ssd_states",
    )(xact, xact, dtsp, dtsp, a128, _head_expand_matrices((0, SSD_HEADS)), s0)


def _ssdy_chunk(x_ref, dt_ref, a_ref, dsk_ref, e_ref, sef_ref, seb_ref, o_ref, y_scr):
    nh = SSD_HEADS
    row = lax.broadcasted_iota(jnp.int32, (Q, Q), 0)
    col = lax.broadcasted_iota(jnp.int32, (Q, Q), 1)
    low = col <= row
    lane = lax.broadcasted_iota(jnp.int32, (1, LANES), 1)
    lane_lo = lane < SSD_HEAD_DIM
    dt = dt_ref[...]
    da = dt * a_ref[...]
    cs_f = _tri_dot(jnp.where(low, 1.0, 0.0).astype(BF16), da)
    cs_b = _tri_dot(jnp.where(col >= row, 1.0, 0.0).astype(BF16), da)
    p2 = jnp.where(lane < nh, cs_f, jnp.where(lane < 2 * nh, cs_b, jnp.log(dt))) * LOG2E
    pt2 = p2.T
    sc = _split_hi_lo(jnp.exp2(p2))
    sc_f = jnp.dot(sc, e_ref[0], preferred_element_type=F32)
    sc_b = jnp.dot(sc, e_ref[1], preferred_element_type=F32)
    hp_per_g = nh // 2 // SSD_GROUPS

    for g in range(SSD_GROUPS):
        bg = x_ref[:, SSD_INNER + g * SSD_STATE:SSD_INNER + (g + 1) * SSD_STATE]
        cg = x_ref[:, SSD_INNER + (SSD_GROUPS + g) * SSD_STATE:SSD_INNER + (SSD_GROUPS + g + 1) * SSD_STATE]
        cb = lax.dot_general(cg, bg, (((1,), (1,)), ((), ())), preferred_element_type=F32)
        gsl = slice(g * SSD_GROUP_W, (g + 1) * SSD_GROUP_W)
        yoff_f = jnp.dot(cg, sef_ref[:, gsl], preferred_element_type=F32)
        yoff_b = jnp.dot(cg, seb_ref[:, gsl], preferred_element_type=F32)
        dcb = jnp.sum(cg.astype(F32) * bg.astype(F32), axis=1, keepdims=True)
        dself = jnp.dot(_split_hi_lo(dcb * dt + dsk_ref[...]), e_ref[2, :, gsl], preferred_element_type=F32)
        for jj in range(hp_per_g):
            j = g * hp_per_g + jj
            sl = slice(j * LANES, (j + 1) * LANES)
            xp = x_ref[:, sl]
            res = []
            for h in (2 * j, 2 * j + 1):
                r_f = pt2[h:h + 1, :] - pt2[2 * nh + h:2 * nh + h + 1, :]
                r_b = pt2[nh + h:nh + h + 1, :] - pt2[3 * nh + h:3 * nh + h + 1, :]
                arg = jnp.where(low, p2[:, h:h + 1] - r_f, p2[:, nh + h:nh + h + 1] - r_b)
                m = (cb * jnp.exp2(arg)).astype(BF16)
                res.append(jnp.dot(m, xp, preferred_element_type=F32))
            y_scr[:, sl] = jnp.where(lane_lo, res[0], res[1])
        o_ref[:, gsl] = (y_scr[:, gsl] + sc_f[:, gsl] * yoff_f + sc_b[:, gsl] * yoff_b
                         + dself * x_ref[:, gsl].astype(F32)).astype(BF16)


def _ssdy_kernel(x_ref, dt_ref, a_ref, dsk_ref, e_ref, sef_ref, seb_ref, o_ref, y_scr):
    for sub in range(CPS):
        rows = pl.ds(sub * Q, Q)
        _ssdy_chunk(x_ref.at[rows], dt_ref.at[rows], a_ref, dsk_ref, e_ref, sef_ref.at[sub], seb_ref.at[sub],
                    o_ref.at[rows], y_scr.at[sub])


def _ssdy_call(xact, dtsp, a128, dskip, sef, seb):
    b, l, _ = xact.shape
    ns = l // (CPS * Q)
    qs = CPS * Q
    return pl.pallas_call(
        _ssdy_kernel,
        grid=(b, ns),
        in_specs=[pl.BlockSpec((None, qs, SSD_CONV_DIM), lambda bi, ci: (bi, ci, 0)),
                  pl.BlockSpec((None, qs, DT_W), lambda bi, ci: (bi, ci, 0)),
                  pl.BlockSpec((1, DT_W), lambda bi, ci: (0, 0)),
                  pl.BlockSpec((1, DT_W), lambda bi, ci: (0, 0)),
                  pl.BlockSpec((3, 2 * DT_W, SSD_INNER), lambda bi, ci: (0, 0, 0)),
                  pl.BlockSpec((None, CPS, SSD_STATE, SSD_INNER), lambda bi, ci: (bi, ci, 0, 0)),
                  pl.BlockSpec((None, CPS, SSD_STATE, SSD_INNER), lambda bi, ci: (bi, ci, 0, 0))],
        out_specs=pl.BlockSpec((None, qs, SSD_INNER), lambda bi, ci: (bi, ci, 0)),
        out_shape=jax.ShapeDtypeStruct((b, l, SSD_INNER), BF16),
        scratch_shapes=[pltpu.VMEM((CPS, Q, SSD_INNER), F32)],
        compiler_params=_cparams(("arbitrary", "arbitrary")),
        name="ssd_y",
    )(xact, dtsp, a128, dskip, _head_expand_matrices((0, SSD_HEADS, 3 * SSD_HEADS)), sef, seb)


def _merge_kernel(ya_ref, lg_ref, y_ref, z_ref, gt_ref, x_ref, mod_ref, snw_ref, wl_ref, ws_ref, wo_ref, o_ref):
    ra_in = (ya_ref[...].astype(F32) * _gelu_tanh(lg_ref[...].astype(F32))).astype(BF16)
    ra = jnp.dot(ra_in, wl_ref[...], preferred_element_type=F32)
    gn = []
    for g in range(SSD_GROUPS):
        gsl = slice(g * SSD_GROUP_W, (g + 1) * SSD_GROUP_W)
        zg = z_ref[:, gsl].astype(F32)
        gated = y_ref[:, gsl].astype(F32) * (zg * _sigmoid_tanh(zg))
        ms = jnp.mean(gated * gated, axis=-1, keepdims=True)
        gn.append((gated * lax.rsqrt(ms + EPS) * snw_ref[:, gsl]).astype(BF16))
    rb = jnp.dot(jnp.concatenate(gn, axis=1), ws_ref[...], preferred_element_type=F32)
    ga = _sigmoid_tanh(gt_ref[:, :D_MODEL].astype(F32))
    gb = _sigmoid_tanh(gt_ref[:, D_MODEL:].astype(F32))
    m = (ga * ra + gb * rb).astype(BF16)
    o = jnp.dot(m, wo_ref[...], preferred_element_type=F32)
    o_ref[...] = x_ref[...] + mod_ref[2:3, :] * o


def _merge_call(ya, y, big, x, mod, snw, wl, ws, wo):
    b, l, d = x.shape
    tm = min(512, l)
    const = lambda bi, i: (0, 0)
    single = pl.Buffered(1)
    return pl.pallas_call(
        _merge_kernel,
        grid=(b, l // tm),
        in_specs=[pl.BlockSpec((None, tm, LRU_WIDTH), lambda bi, i: (bi, i, 0)),
                  pl.BlockSpec((None, tm, LRU_WIDTH), lambda bi, i: (bi, i, OFF_LG // LRU_WIDTH)),
                  pl.BlockSpec((None, tm, SSD_INNER), lambda bi, i: (bi, i, 0)),
                  pl.BlockSpec((None, tm, SSD_INNER), lambda bi, i: (bi, i, OFF_Z // SSD_INNER)),
                  pl.BlockSpec((None, tm, 2 * d), lambda bi, i: (bi, i, OFF_GT // (2 * D_MODEL))),
                  pl.BlockSpec((None, tm, d), lambda bi, i: (bi, i, 0)),
                  pl.BlockSpec((None, 6, d), lambda bi, i: (bi, 0, 0)),
                  pl.BlockSpec((1, SSD_INNER), const),
                  pl.BlockSpec((LRU_WIDTH, d), const, pipeline_mode=single),
                  pl.BlockSpec((SSD_INNER, d), const, pipeline_mode=single),
                  pl.BlockSpec((d, d), const, pipeline_mode=single)],
        out_specs=pl.BlockSpec((None, tm, d), lambda bi, i: (bi, i, 0)),
        out_shape=jax.ShapeDtypeStruct((b, l, d), F32),
        compiler_params=_cparams(("arbitrary", "arbitrary")),
        name="merge",
    )(ya, big, y, big, big, x, mod, snw, wl, ws, wo)


def _ffn_kernel(*refs, tm, gw, halo, final_norm):
    if halo:
        (x_ref, xp_ref, xn_ref, nw_ref, mod_ref, wu_ref, cw_ref, cb_ref, wd_ref, fnw_ref,
         o_ref, h_scr, u_scr, v_scr, act_scr) = refs
    else:
        (x_ref, nw_ref, mod_ref, wu_ref, cw_ref, cb_ref, wd_ref, fnw_ref,
         o_ref, h_scr, u_scr, v_scr, act_scr) = refs
    i = pl.program_id(1)
    n_i = pl.num_programs(1)
    nw = nw_ref[...]
    shift = mod_ref[3:4, :]
    scale = mod_ref[4:5, :]
    ext = tm + 2 * gw if halo else tm
    top = gw if halo else 0
    pad = SUBLANES

    x = x_ref[...]
    h_scr[top:top + tm, :] = _norm_mod(x, nw, shift, scale).astype(BF16)
    if halo:
        hp = _norm_mod(xp_ref[...], nw, shift, scale)
        hn = _norm_mod(xn_ref[...], nw, shift, scale)
        h_scr[0:gw, :] = jnp.where(i > 0, hp, 0.0).astype(BF16)
        h_scr[gw + tm:ext, :] = jnp.where(i < n_i - 1, hn, 0.0).astype(BF16)
    for slot in range(2):
        u_scr[slot, 0:pad, :] = jnp.zeros((pad, FCH), F32)
        u_scr[slot, pad + ext:pad + ext + pad, :] = jnp.zeros((pad, FCH), F32)

    def up(ci, slot):
        c0 = pl.multiple_of(ci * FCH, FCH)
        u_scr[slot, pad:pad + ext, :] = jnp.dot(h_scr[...], wu_ref[:, pl.ds(c0, FCH)],
                                                preferred_element_type=F32)
        v_scr[slot] = jnp.dot(h_scr[top:top + tm, :], wu_ref[:, pl.ds(FFN_DIM + c0, FCH)],
                              preferred_element_type=F32)

    def down(ci, slot):
        c0 = pl.multiple_of(ci * FCH, FCH)
        colid = lax.broadcasted_iota(jnp.int32, (ext, FCH), 0) & (gw - 1)
        u = u_scr[slot, pad:pad + ext, :]
        ul = jnp.where(colid >= 1, u_scr[slot, pad - 1:pad - 1 + ext, :], 0.0)
        ur = jnp.where(colid <= gw - 2, u_scr[slot, pad + 1:pad + 1 + ext, :], 0.0)
        cw = cw_ref[:, pl.ds(c0, FCH)]
        acc = cb_ref[:, pl.ds(c0, FCH)]
        for dr in ((-1, 0, 1) if halo else (0,)):
            o = top + dr * gw
            k = (dr + 1) * 3
            acc = (acc + cw[k:k + 1, :] * ul[o:o + tm] + cw[k + 1:k + 2, :] * u[o:o + tm]
                   + cw[k + 2:k + 3, :] * ur[o:o + tm])
        act_scr[:, pl.ds(c0, FCH)] = (_gelu_tanh(acc) * v_scr[slot]).astype(BF16)

    n_ch = FFN_DIM // FCH
    up(0, 0)
    acc = None
    k0 = 0
    for c in range(n_ch):
        if c + 1 < n_ch:
            up(c + 1, (c + 1) % 2)
        down(c, c % 2)
        if (c + 1) in DOWN_SPLITS:
            k1 = (c + 1) * FCH
            part = jnp.dot(act_scr[:, k0:k1], wd_ref[k0:k1, :], preferred_element_type=F32)
            acc = part if acc is None else acc + part
            k0 = k1
    y = x + mod_ref[5:6, :] * acc
    if final_norm:
        ms = jnp.mean(y * y, axis=-1, keepdims=True)
        y = y * lax.rsqrt(ms + EPS) * fnw_ref[...]
    o_ref[...] = y


def _ffn_call(x, nw, mod, wu, cw, cb, wd, fnw, *, gw, halo, final_norm):
    b, l, d = x.shape
    tm = min(1024, l)
    assert tm % gw == 0 and gw & (gw - 1) == 0
    ext = tm + 2 * gw if halo else tm
    r = tm // gw
    ng = l // gw
    const = lambda bi, i: (0, 0)
    single = pl.Buffered(1)
    in_specs = [pl.BlockSpec((None, tm, d), lambda bi, i: (bi, i, 0))]
    args = [x]
    if halo:
        in_specs += [pl.BlockSpec((None, gw, d), lambda bi, i: (bi, jnp.maximum(i * r - 1, 0), 0)),
                     pl.BlockSpec((None, gw, d), lambda bi, i: (bi, jnp.minimum((i + 1) * r, ng - 1), 0))]
        args += [x, x]
    in_specs += [pl.BlockSpec((1, d), const),
                 pl.BlockSpec((None, 6, d), lambda bi, i: (bi, 0, 0)),
                 pl.BlockSpec((d, 2 * FFN_DIM), const, pipeline_mode=single),
                 pl.BlockSpec((9, FFN_DIM), const),
                 pl.BlockSpec((1, FFN_DIM), const),
                 pl.BlockSpec((FFN_DIM, d), const, pipeline_mode=single),
                 pl.BlockSpec((1, d), const)]
    args += [nw, mod, wu, cw, cb, wd, fnw]
    kern = functools.partial(_ffn_kernel, tm=tm, gw=gw, halo=halo, final_norm=final_norm)
    return pl.pallas_call(
        kern,
        grid=(b, l // tm),
        in_specs=in_specs,
        out_specs=pl.BlockSpec((None, tm, d), lambda bi, i: (bi, i, 0)),
        out_shape=jax.ShapeDtypeStruct((b, l, d), F32),
        scratch_shapes=[pltpu.VMEM((ext, d), BF16),
                        pltpu.VMEM((2, ext + 2 * SUBLANES, FCH), F32),
                        pltpu.VMEM((2, tm, FCH), F32),
                        pltpu.VMEM((tm, FFN_DIM), BF16)],
        compiler_params=_cparams(("arbitrary", "arbitrary")),
        name="ffn",
    )(*args)


def _gate_weights(wa, wx):
    hpb = CBLK // LRU_HEAD_DIM
    nblk = LRU_WIDTH // CBLK
    eye = jnp.eye(hpb, dtype=F32)

    def blockdiag(w):
        w = w.reshape(nblk, hpb, LRU_HEAD_DIM, LRU_HEAD_DIM)
        bd = jnp.einsum('bhij,hk->bhikj', w, eye)
        return bd.reshape(nblk, CBLK, CBLK)

    parts = [blockdiag(wa[0]), blockdiag(wx[0]), blockdiag(wa[1]), blockdiag(wx[1])]
    return jnp.concatenate(parts, axis=-1).astype(BF16)


def _gate_biases(ba, bx):
    nblk = LRU_WIDTH // CBLK
    parts = [v.reshape(nblk, 1, CBLK) for v in (ba[0], bx[0], ba[1], bx[1])]
    return jnp.concatenate(parts, axis=-1)


def kernel(x, c, ctx, c_ctx, ada_w, ada_b, norm_mix_w, norm_ffn_w, w_in, lru_conv_w, lru_conv_b, lru_wa, lru_ba, lru_wx, lru_bx, lru_lambda, lru_proj, ssd_conv_w, ssd_conv_b, ssd_dt_bias, ssd_a_log, ssd_d, ssd_norm_w, ssd_proj, w_out, ffn_w_up, ffn_conv_w, ffn_conv_b, ffn_w_down, final_norm_w):
    b, l, d = x.shape
    lc = ctx.shape[1]
    depth = ada_w.shape[0]
    gw_x = 64
    assert d == D_MODEL and b + 1 <= 16 and l % (CPS * Q) == 0 and lc % (CPS * Q) == 0

    s = jnp.zeros((16, d), F32).at[:b].set(c).at[b].set(c_ctx)
    mods = _ada_call(s, ada_w, ada_b)

    o_lx, o_lg, o_z, o_xbc, o_dt, o_gt = 0, 1024, 2048, 4096, 7168, 7232
    h0 = jnp.zeros((b, 2, LRU_WIDTH), F32)
    s0 = jnp.zeros((b, 2, SSD_STATE, SSD_INNER), F32)
    fnw = final_norm_w.reshape(1, d)
    ctxf = ctx.reshape(1, b * lc, d)

    for li in range(depth):
        last = li == depth - 1
        modx = mods[li, :b].reshape(b, 6, d)
        modc = mods[li, b].reshape(1, 6, d)
        wl = w_in[li]
        w_big = jnp.concatenate([wl[:, o_z:o_xbc], wl[:, o_gt:], wl[:, o_lx:o_z], wl[:, o_xbc:o_dt]],
                                axis=1).astype(BF16)
        w_dt = jnp.concatenate([wl[:, o_dt:o_gt], wl[:, o_dt:o_gt]], axis=1).astype(BF16)
        nmw = norm_mix_w[li].reshape(1, d)
        nfw = norm_ffn_w[li].reshape(1, d)
        wg = _gate_weights(lru_wa[li], lru_wx[li])
        bgs = _gate_biases(lru_ba[li], lru_bx[li])
        lcw, lcb = lru_conv_w[li], lru_conv_b[li].reshape(1, -1)
        scw, scb = ssd_conv_w[li], ssd_conv_b[li].reshape(1, -1)
        dtb = jnp.tile(ssd_dt_bias[li].reshape(1, -1), (1, 2))
        a128 = jnp.tile(-jnp.exp(ssd_a_log[li].astype(F32)).reshape(1, -1), (1, 2))
        dsk = jnp.concatenate([jnp.zeros((1, DT_W - SSD_HEADS), F32), ssd_d[li].reshape(1, -1)], axis=1)
        snw = ssd_norm_w[li].reshape(1, -1)
        wlp = lru_proj[li].astype(BF16)
        wsp = ssd_proj[li].astype(BF16)
        wo = w_out[li].astype(BF16)
        wu = ffn_w_up[li].astype(BF16)
        wd = ffn_w_down[li].astype(BF16)
        fcw = ffn_conv_w[li].reshape(9, FFN_DIM)
        fcb = ffn_conv_b[li].reshape(1, FFN_DIM)

        big_cf, dt_cf = _inproj_call(ctxf, nmw, modc, w_big, w_dt)
        big_c = big_cf.reshape(b, lc, BIG_W)
        dt_c = dt_cf.reshape(b, lc, DT_W)
        ra_c, hfin_c = _lru_call(big_c, lcw, lcb, wg, bgs, lru_lambda[li], h0)
        xact_c, dtsp_c = _ssdprep_call(big_c, dt_c, scw, scb, dtb)
        sef_c, seb_c, sfin_c = _ssdstate_call(xact_c, dtsp_c, a128, s0)

        big_x, dt_x = _inproj_call(x, nmw, modx, w_big, w_dt)
        ra_x, _ = _lru_call(big_x, lcw, lcb, wg, bgs, lru_lambda[li], hfin_c)
        xact_x, dtsp_x = _ssdprep_call(big_x, dt_x, scw, scb, dtb)
        sef_x, seb_x, _ = _ssdstate_call(xact_x, dtsp_x, a128, sfin_c)
        y_x = _ssdy_call(xact_x, dtsp_x, a128, dsk, sef_x, seb_x)
        x = _merge_call(ra_x, y_x, big_x, x, modx, snw, wlp, wsp, wo)
        x = _ffn_call(x, nfw, modx, wu, fcw, fcb, wd, fnw, gw=gw_x, halo=True, final_norm=last)

        if not last:
            y_c = _ssdy_call(xact_c, dtsp_c, a128, dsk, sef_c, seb_c)
            ctxf = _merge_call(ra_c.reshape(1, b * lc, LRU_WIDTH), y_c.reshape(1, b * lc, SSD_INNER),
                               big_cf, ctxf, modc, snw, wlp, wsp, wo)
            ctxf = _ffn_call(ctxf, nfw, modc, wu, fcw, fcb, wd, fnw, gw=lc, halo=False, final_norm=False)
    return x
```

```python
import functools

import jax
import jax.numpy as jnp
from jax import lax
from jax.experimental import pallas as pl
from jax.experimental.pallas import tpu as pltpu

F32 = jnp.float32
BF16 = jnp.bfloat16

EPS = 1e-6
LOG2E = 1.4426950408889634
D_MODEL = 1024
LRU_WIDTH = 1024
LRU_HEADS = 16
LRU_HEAD_DIM = 64
LRU_C = 8.0
SSD_INNER = 2048
SSD_HEADS = 32
SSD_HEAD_DIM = 64
SSD_GROUPS = 4
SSD_STATE = 128
SSD_CONV_DIM = SSD_INNER + 2 * SSD_GROUPS * SSD_STATE
SSD_GROUP_W = SSD_INNER // SSD_GROUPS
FFN_DIM = 2816

LANES = 128
SUBLANES = 8
VMEM_LIMIT_BYTES = 56 * 1024 * 1024

OFF_Z = 0
OFF_GT = OFF_Z + SSD_INNER
OFF_LX = OFF_GT + 2 * D_MODEL
OFF_LG = OFF_LX + LRU_WIDTH
OFF_XBC = OFF_LG + LRU_WIDTH
BIG_W = OFF_XBC + SSD_CONV_DIM
INPROJ_TN = 1536
NBIG_TILES = OFF_XBC // INPROJ_TN
DT_W = 128

CBLK = 256
LRU_CBLK = 512
LRU_PITCH = 40
CONV_HALO = 16
Q = 128
CPS = 2
FCH = 256
DOWN_SPLITS = (6, 11)
assert Q == LANES and DOWN_SPLITS[-1] == FFN_DIM // FCH


def _cparams(sem):
    return pltpu.CompilerParams(dimension_semantics=sem, vmem_limit_bytes=VMEM_LIMIT_BYTES)


def _gelu_tanh(x):
    t = jnp.tanh(x * (0.7978845608028654 + (0.7978845608028654 * 0.044715) * (x * x)))
    hx = 0.5 * x
    return hx + hx * t


def _sigmoid(x):
    return 1.0 / (1.0 + jnp.exp(-x))


def _sigmoid_tanh(x):
    return 0.5 * jnp.tanh(0.5 * x) + 0.5


def _norm_mod(x, nw, shift, scale):
    ms = jnp.mean(x * x, axis=-1, keepdims=True)
    y = x * lax.rsqrt(ms + EPS) * nw
    return y * (1.0 + scale) + shift


def _pair_expand(v, j, lane_lo, off=0):
    return jnp.where(lane_lo, v[:, off + 2 * j:off + 2 * j + 1], v[:, off + 2 * j + 1:off + 2 * j + 2])


def _tri_dot(tri_bf16, v):
    hi = v.astype(BF16)
    r1 = v - hi.astype(F32)
    mid = r1.astype(BF16)
    lo = (r1 - mid.astype(F32)).astype(BF16)
    return (jnp.dot(tri_bf16, hi, preferred_element_type=F32)
            + jnp.dot(tri_bf16, mid, preferred_element_type=F32)
            + jnp.dot(tri_bf16, lo, preferred_element_type=F32))


def _ada_kernel(s_ref, w_ref, b_ref, o_ref):
    s = s_ref[...]
    s = s * _sigmoid(s)
    o_ref[...] = jnp.dot(s, w_ref[...], preferred_element_type=F32,
                         precision=lax.Precision.HIGHEST) + b_ref[...]


def _ada_call(s, ada_w, ada_b):
    depth, d, n = ada_w.shape
    tn = 1536
    rows = s.shape[0]
    return pl.pallas_call(
        _ada_kernel,
        grid=(depth, n // tn),
        in_specs=[pl.BlockSpec((rows, d), lambda l, j: (0, 0)),
                  pl.BlockSpec((None, d, tn), lambda l, j: (l, 0, j)),
                  pl.BlockSpec((None, 1, tn), lambda l, j: (l, 0, j))],
        out_specs=pl.BlockSpec((None, rows, tn), lambda l, j: (l, 0, j)),
        out_shape=jax.ShapeDtypeStruct((depth, rows, n), F32),
        compiler_params=_cparams(("arbitrary", "arbitrary")),
        name="ada",
    )(s, ada_w, ada_b.reshape(depth, 1, n))


def _inproj_kernel(x_ref, xp_ref, xn_ref, nw_ref, mod_ref, w_ref, wdt_ref, dtb_ref, cw_ref, cb_ref,
                   big_ref, dt_ref, h_scr, *, tm, tn, seq, rt):
    i = pl.program_id(1)
    n_i = pl.num_programs(1)
    j = pl.program_id(2)
    hr = CONV_HALO

    @pl.when(j == 0)
    def _():
        nw, shift, scale = nw_ref[...], mod_ref[0:1, :], mod_ref[1:2, :]
        h = _norm_mod(x_ref[...], nw, shift, scale).astype(BF16)
        h_scr[hr:hr + tm, :] = h
        h_scr[0:hr, :] = jnp.where(i > 0, _norm_mod(xp_ref[...], nw, shift, scale), 0.0).astype(BF16)
        h_scr[hr + tm:hr + tm + hr, :] = jnp.where(i < n_i - 1, _norm_mod(xn_ref[...], nw, shift, scale), 0.0).astype(BF16)
        v = jnp.dot(h, wdt_ref[...], preferred_element_type=F32) + dtb_ref[...]
        dt_ref[...] = jnp.maximum(v, 0.0) + jnp.log1p(jnp.exp(-jnp.abs(v)))

    @pl.when(j < NBIG_TILES)
    def _():
        big_ref[...] = jnp.dot(h_scr[hr:hr + tm, :], w_ref[...], preferred_element_type=F32).astype(big_ref.dtype)

    @pl.when(j >= NBIG_TILES)
    def _():
        n = rt + 2 * hr

        for ri in range(tm // rt):
            r0 = ri * rt
            res = jnp.dot(h_scr[r0:r0 + n, :], w_ref[...], preferred_element_type=F32)
            acc = cb_ref[...] + cw_ref[2:3, :] * res[hr:hr + rt]
            for k in (0, 1, 3):
                sh = pltpu.roll(res, (2 - k) % n, 0)[hr:hr + rt]
                if seq < tm:
                    pos = (lax.broadcasted_iota(jnp.int32, (rt, tn), 0) + r0) & (seq - 1)
                    sh = jnp.where((pos + (k - 2) >= 0) & (pos + (k - 2) < seq), sh, 0.0)
                acc = acc + cw_ref[k:k + 1, :] * sh
            big_ref[r0:r0 + rt, :] = (acc * _sigmoid_tanh(acc)).astype(big_ref.dtype)


def _inproj_call(x, nw, mod, w, wdt, dtb, cw, cb, *, seq):
    b, l, d = x.shape
    tm = min(2048, l)
    tn = INPROJ_TN
    rt = min(512, tm)
    hr = CONV_HALO
    nh = l // hr
    r = tm // hr
    assert seq & (seq - 1) == 0 and (seq >= tm or tm == l) and OFF_XBC % tn == 0 and BIG_W % tn == 0
    kern = functools.partial(_inproj_kernel, tm=tm, tn=tn, seq=seq, rt=rt)
    return pl.pallas_call(
        kern,
        grid=(b, l // tm, BIG_W // tn),
        in_specs=[pl.BlockSpec((None, tm, d), lambda bi, i, j: (bi, i, 0)),
                  pl.BlockSpec((None, hr, d), lambda bi, i, j: (bi, jnp.maximum(i * r - 1, 0), 0)),
                  pl.BlockSpec((None, hr, d), lambda bi, i, j: (bi, jnp.minimum((i + 1) * r, nh - 1), 0)),
                  pl.BlockSpec((1, d), lambda bi, i, j: (0, 0)),
                  pl.BlockSpec((None, 6, d), lambda bi, i, j: (bi, 0, 0)),
                  pl.BlockSpec((d, tn), lambda bi, i, j: (0, j)),
                  pl.BlockSpec((d, DT_W), lambda bi, i, j: (0, 0)),
                  pl.BlockSpec((1, DT_W), lambda bi, i, j: (0, 0)),
                  pl.BlockSpec((4, tn), lambda bi, i, j: (0, jnp.maximum(j - NBIG_TILES, 0))),
                  pl.BlockSpec((1, tn), lambda bi, i, j: (0, jnp.maximum(j - NBIG_TILES, 0)))],
        out_specs=[pl.BlockSpec((None, tm, tn), lambda bi, i, j: (bi, i, j)),
                   pl.BlockSpec((None, tm, DT_W), lambda bi, i, j: (bi, i, 0))],
        out_shape=[jax.ShapeDtypeStruct((b, l, BIG_W), BF16),
                   jax.ShapeDtypeStruct((b, l, DT_W), F32)],
        scratch_shapes=[pltpu.VMEM((tm + 2 * hr, d), BF16)],
        compiler_params=_cparams(("arbitrary", "arbitrary", "arbitrary")),
        name="in_proj",
    )(x, x, x, nw, mod, w, wdt, dtb, cw, cb)


def _conv_shift_matrix(t, permuted):
    q = jnp.arange(t, dtype=jnp.int32)
    tq = (q % SUBLANES) * (t // SUBLANES) + q // SUBLANES if permuted else q
    src = tq[None, :, None] + jnp.arange(4, dtype=jnp.int32)[:, None, None] - 2
    return (src == jnp.arange(t, dtype=jnp.int32)[None, None, :]).reshape(4 * t, t).astype(BF16)


def _conv4_shift(x_ref, s_ref, sh_scr, i, slot, t):
    r0 = pl.multiple_of(i * t, t)
    sh_scr[slot] = jnp.dot(s_ref[...], x_ref[pl.ds(r0, t), :], preferred_element_type=F32)


def _conv4_tile(x_ref, sh_scr, slot, cw_ref, cb_ref, i, n_tiles, t, l, permuted):
    hr = CONV_HALO
    r0 = pl.multiple_of(i * t, t)
    acc = cb_ref[...] + cw_ref[0:1, :] * sh_scr[slot, 0:t, :]
    for k in range(1, 4):
        acc = acc + cw_ref[k:k + 1, :] * sh_scr[slot, k * t:(k + 1) * t, :]
    prev = x_ref[pl.ds(pl.multiple_of(jnp.maximum(r0 - hr, 0), hr), hr), :].astype(F32)
    nxt = x_ref[pl.ds(pl.multiple_of(jnp.minimum(r0 + t, l - hr), hr), hr), :].astype(F32)
    pm2 = jnp.where(i > 0, prev[hr - 2:hr - 1], 0.0)
    pm1 = jnp.where(i > 0, prev[hr - 1:hr], 0.0)
    np0 = jnp.where(i < n_tiles - 1, nxt[0:1], 0.0)
    row8 = lax.broadcasted_iota(jnp.int32, (SUBLANES, acc.shape[1]), 0)
    f0 = cw_ref[0:1, :] * pm2 + cw_ref[1:2, :] * pm1
    f1 = cw_ref[0:1, :] * pm1
    fl = cw_ref[3:4, :] * np0
    sl = SUBLANES
    if permuted:
        head = [acc[0:sl] + jnp.where(row8 == 0, f0, 0.0), acc[sl:2 * sl] + jnp.where(row8 == 0, f1, 0.0)]
    else:
        head = [acc[0:sl] + jnp.where(row8 == 0, f0, jnp.where(row8 == 1, f1, 0.0)), acc[sl:2 * sl]]
    tail = acc[t - sl:t] + jnp.where(row8 == sl - 1, fl, 0.0)
    return jnp.concatenate(head + [acc[2 * sl:t - sl], tail], axis=0)


def _pipelined_tiles(n_tiles, shift, body, carry):
    shift(0, 0)
    if n_tiles == 1:
        return body(0, 0, carry)
    assert n_tiles % 2 == 0

    def pair(ip, c):
        i = 2 * ip
        shift(i + 1, 1)
        c = body(i, 0, c)
        shift(jnp.minimum(i + 2, n_tiles - 1), 0)
        return body(i + 1, 1, c)

    return lax.fori_loop(0, n_tiles // 2, pair, carry)


def _lru_kernel(lx_ref, s_ref, cw_ref, cb_ref, wg_ref, bg_ref, lam_ref, h0_ref,
                o_ref, hfin_ref, u_scr, y_scr, perm_scr, sh_scr, *, l, t):
    n_tiles = l // t
    c = LRU_CBLK
    gw = CBLK
    seg = t // SUBLANES
    nslab = c // LANES
    lam = lam_ref[...]
    c8 = LRU_C * (jnp.minimum(lam, 0.0) - jnp.log1p(jnp.exp(-jnp.abs(lam))))
    row8 = lax.broadcasted_iota(jnp.int32, (SUBLANES, c), 0)

    def unpermute(val):
        for j in range(seg):
            for k in range(nslab):
                perm_scr[k, pl.ds(j, SUBLANES, stride=LRU_PITCH), :] = val[j * SUBLANES:(j + 1) * SUBLANES, k * LANES:(k + 1) * LANES]
        segs = [jnp.concatenate([perm_scr[k, s * LRU_PITCH:s * LRU_PITCH + seg, :] for k in range(nslab)], axis=1)
                for s in range(SUBLANES)]
        return jnp.concatenate(segs, axis=0)

    def gates(u, d):
        ub = u.astype(BF16)
        gs = [jnp.dot(ub[:, q * gw:(q + 1) * gw], wg_ref[q, :, d * 2 * gw:(d + 1) * 2 * gw],
                      preferred_element_type=F32) + bg_ref[q, :, d * 2 * gw:(d + 1) * 2 * gw] for q in range(c // gw)]
        r = _sigmoid_tanh(jnp.concatenate([g[:, :gw] for g in gs], axis=1))
        ig = _sigmoid_tanh(jnp.concatenate([g[:, gw:] for g in gs], axis=1))
        log_a = c8[d:d + 1, :] * r
        a = jnp.exp(log_a)
        s = -jnp.tanh(log_a) * (a * a + 1.0)
        bb = jnp.where(s > 0.0, s * lax.rsqrt(s), 0.0) * (ig * u)
        return a, bb

    def scan(a, bb, hc, fwd):
        order = list(range(seg)) if fwd else list(range(seg - 1, -1, -1))
        hl, pr = [None] * seg, [None] * seg
        prev = None
        for j in order:
            aj = a[j * SUBLANES:(j + 1) * SUBLANES]
            bj = bb[j * SUBLANES:(j + 1) * SUBLANES]
            if prev is None:
                hl[j], pr[j] = bj, aj
            else:
                hl[j], pr[j] = aj * hl[prev] + bj, aj * pr[prev]
            prev = j
        av, bv = pr[prev], hl[prev]
        for k in (1, 2, 4):
            sh = k if fwd else SUBLANES - k
            m = (row8 >= k) if fwd else (row8 < SUBLANES - k)
            a_s = pltpu.roll(av, sh, 0)
            b_s = pltpu.roll(bv, sh, 0)
            bv = jnp.where(m, av * b_s + bv, bv)
            av = jnp.where(m, av * a_s, av)
        e = bv + av * hc
        if fwd:
            cin = jnp.where(row8 == 0, hc, pltpu.roll(e, 1, 0))
            hout = e[SUBLANES - 1:SUBLANES, :]
        else:
            cin = jnp.where(row8 == SUBLANES - 1, hc, pltpu.roll(e, SUBLANES - 1, 0))
            hout = e[0:1, :]
        h = jnp.concatenate([hl[j] + pr[j] * cin for j in range(seg)], axis=0)
        return h, hout

    def fwd_tile(i, slot, hprev):
        r0 = pl.multiple_of(i * t, t)
        up = _conv4_tile(lx_ref, sh_scr, slot, cw_ref, cb_ref, i, n_tiles, t, l, True)
        u_scr[pl.ds(r0, t), :] = up
        a, bb = gates(up, 0)
        h, hout = scan(a, bb, hprev, True)
        y_scr[pl.ds(r0, t), :] = h
        return hout

    hf = _pipelined_tiles(n_tiles, lambda i, slot: _conv4_shift(lx_ref, s_ref, sh_scr, i, slot, t),
                          fwd_tile, h0_ref[0:1, :])
    hfin_ref[0:1, :] = hf

    def bwd_tile(ii, hnext):
        i = n_tiles - 1 - ii
        r0 = pl.multiple_of(i * t, t)
        a, bb = gates(u_scr[pl.ds(r0, t), :], 1)
        h, hout = scan(a, bb, hnext, False)
        o_ref[pl.ds(r0, t), :] = unpermute(y_scr[pl.ds(r0, t), :] + h).astype(BF16)
        return hout

    hb = lax.fori_loop(0, n_tiles, bwd_tile, h0_ref[1:2, :])
    hfin_ref[1:2, :] = hb


def _lru_call(big, cw, cb, wg, bg, lam, h0):
    b, l, _ = big.shape
    t = 256
    c = LRU_CBLK
    nblk = LRU_WIDTH // c
    gpb = c // CBLK
    lx0 = OFF_LX // c
    kern = functools.partial(_lru_kernel, l=l, t=t)
    return pl.pallas_call(
        kern,
        grid=(b, nblk),
        in_specs=[pl.BlockSpec((None, l, c), lambda bi, ci: (bi, 0, lx0 + ci)),
                  pl.BlockSpec((4 * t, t), lambda bi, ci: (0, 0)),
                  pl.BlockSpec((4, c), lambda bi, ci: (0, ci)),
                  pl.BlockSpec((1, c), lambda bi, ci: (0, ci)),
                  pl.BlockSpec((gpb, CBLK, 4 * CBLK), lambda bi, ci: (ci, 0, 0)),
                  pl.BlockSpec((gpb, 1, 4 * CBLK), lambda bi, ci: (ci, 0, 0)),
                  pl.BlockSpec((2, c), lambda bi, ci: (0, ci)),
                  pl.BlockSpec((None, 2, c), lambda bi, ci: (bi, 0, ci))],
        out_specs=[pl.BlockSpec((None, l, c), lambda bi, ci: (bi, 0, ci)),
                   pl.BlockSpec((None, 2, c), lambda bi, ci: (bi, 0, ci))],
        out_shape=[jax.ShapeDtypeStruct((b, l, LRU_WIDTH), BF16),
                   jax.ShapeDtypeStruct((b, 2, LRU_WIDTH), F32)],
        scratch_shapes=[pltpu.VMEM((l, c), F32), pltpu.VMEM((l, c), F32),
                        pltpu.VMEM((c // LANES, SUBLANES * LRU_PITCH, LANES), F32),
                        pltpu.VMEM((2, 4 * t, c), F32)],
        compiler_params=_cparams(("arbitrary", "arbitrary")),
        name="lru",
    )(big, _conv_shift_matrix(t, True), cw, cb, wg, bg, lam, h0)


def _ssdstate_kernel(xf_ref, xb_ref, dtf_ref, dtb_ref, a_ref, e_ref, s0_ref,
                     sef_ref, seb_ref, sfin_ref, sf_scr, sb_scr, xw_scr):
    c = pl.program_id(1)
    nc = pl.num_programs(1)

    @pl.when(c == 0)
    def _():
        sf_scr[...] = s0_ref[0]
        sb_scr[...] = s0_ref[1]

    a = a_ref[...]
    row = lax.broadcasted_iota(jnp.int32, (Q, Q), 0)
    col = lax.broadcasted_iota(jnp.int32, (Q, Q), 1)
    lane_lo = lax.broadcasted_iota(jnp.int32, (1, LANES), 1) < SSD_HEAD_DIM
    hp = SSD_HEADS // 2
    gp = hp // SSD_GROUPS

    for d in range(2):
        x_ref = (xf_ref, xb_ref)[d]
        dt_ref = (dtf_ref, dtb_ref)[d]
        off = d * SSD_HEADS
        s_scr = (sf_scr, sb_scr)[d]
        se_ref = (sef_ref, seb_ref)[d]
        tri = (jnp.where(col <= row, 1.0, 0.0) if d == 0 else jnp.where(col >= row, 1.0, 0.0)).astype(BF16)
        for kk in range(CPS):
            sub = kk if d == 0 else CPS - 1 - kk
            rows = slice(sub * Q, (sub + 1) * Q)
            dt = dt_ref[rows, :]
            cs = _tri_dot(tri, dt * a)
            tot = cs[Q - 1:Q, :] if d == 0 else cs[0:1, :]
            w = jnp.exp(tot - cs) * dt
            dec = jnp.exp(tot)
            wexp = jnp.dot(_split_hi_lo(w), e_ref[d], preferred_element_type=F32)
            slot = d * CPS + kk
            xw_scr[slot] = (x_ref[rows, :SSD_INNER].astype(F32) * wexp).astype(BF16)
            se_ref[sub] = s_scr[...].astype(BF16)
            for g in range(SSD_GROUPS):
                bg = x_ref[rows, SSD_INNER + g * SSD_STATE:SSD_INNER + (g + 1) * SSD_STATE]
                sloc = lax.dot_general(bg, xw_scr[slot, :, g * SSD_GROUP_W:(g + 1) * SSD_GROUP_W],
                                       (((0,), (0,)), ((), ())), preferred_element_type=F32)
                for jj in range(gp):
                    j = g * gp + jj
                    sl = slice(j * LANES, (j + 1) * LANES)
                    s_scr[:, sl] = (s_scr[:, sl] * _pair_expand(dec, j, lane_lo, off)
                                    + sloc[:, jj * LANES:(jj + 1) * LANES])

    @pl.when(c == nc - 1)
    def _():
        sfin_ref[0] = sf_scr[...]
        sfin_ref[1] = sb_scr[...]


def _head_expand_matrices(offsets):
    k = jnp.arange(2 * DT_W, dtype=jnp.int32)[:, None] % DT_W
    head = jnp.arange(SSD_INNER, dtype=jnp.int32)[None, :] // SSD_HEAD_DIM
    return jnp.stack([(k == head + off) for off in offsets]).astype(BF16)


def _split_hi_lo(v):
    hi = v.astype(BF16)
    lo = (v - hi.astype(F32)).astype(BF16)
    return jnp.concatenate([hi, lo], axis=1)


def _ssdstate_call(xact, dtsp, a128, s0):
    b, l, _ = xact.shape
    nc = l // Q
    ns = nc // CPS
    qs = CPS * Q
    return pl.pallas_call(
        _ssdstate_kernel,
        grid=(b, ns),
        in_specs=[pl.BlockSpec((None, qs, SSD_CONV_DIM), lambda bi, ci: (bi, ci, OFF_XBC // SSD_CONV_DIM)),
                  pl.BlockSpec((None, qs, SSD_CONV_DIM), lambda bi, ci: (bi, ns - 1 - ci, OFF_XBC // SSD_CONV_DIM)),
                  pl.BlockSpec((None, qs, DT_W), lambda bi, ci: (bi, ci, 0)),
                  pl.BlockSpec((None, qs, DT_W), lambda bi, ci: (bi, ns - 1 - ci, 0)),
                  pl.BlockSpec((1, DT_W), lambda bi, ci: (0, 0)),
                  pl.BlockSpec((2, 2 * DT_W, SSD_INNER), lambda bi, ci: (0, 0, 0)),
                  pl.BlockSpec((None, 2, SSD_STATE, SSD_INNER), lambda bi, ci: (bi, 0, 0, 0))],
        out_specs=[pl.BlockSpec((None, CPS, SSD_STATE, SSD_INNER), lambda bi, ci: (bi, ci, 0, 0)),
                   pl.BlockSpec((None, CPS, SSD_STATE, SSD_INNER), lambda bi, ci: (bi, ns - 1 - ci, 0, 0)),
                   pl.BlockSpec((None, 2, SSD_STATE, SSD_INNER), lambda bi, ci: (bi, 0, 0, 0))],
        out_shape=[jax.ShapeDtypeStruct((b, nc, SSD_STATE, SSD_INNER), BF16),
                   jax.ShapeDtypeStruct((b, nc, SSD_STATE, SSD_INNER), BF16),
                   jax.ShapeDtypeStruct((b, 2, SSD_STATE, SSD_INNER), F32)],
        scratch_shapes=[pltpu.VMEM((SSD_STATE, SSD_INNER), F32), pltpu.VMEM((SSD_STATE, SSD_INNER), F32),
                        pltpu.VMEM((2 * CPS, Q, SSD_INNER), BF16)],
        compiler_params=_cparams(("arbitrary", "arbitrary")),
        name="ssd_states",
    )(xact, xact, dtsp, dtsp, a128, _head_expand_matrices((0, SSD_HEADS)), s0)


def _ssdy_chunk(x_ref, dt_ref, a_ref, dsk_ref, e_ref, sef_ref, seb_ref, o_ref, y_scr):
    nh = SSD_HEADS
    row = lax.broadcasted_iota(jnp.int32, (Q, Q), 0)
    col = lax.broadcasted_iota(jnp.int32, (Q, Q), 1)
    low = col <= row
    lane = lax.broadcasted_iota(jnp.int32, (1, LANES), 1)
    lane_lo = lane < SSD_HEAD_DIM
    dt = dt_ref[...]
    da = dt * a_ref[...]
    cs_f = _tri_dot(jnp.where(low, 1.0, 0.0).astype(BF16), da)
    cs_b = _tri_dot(jnp.where(col >= row, 1.0, 0.0).astype(BF16), da)
    p2 = jnp.where(lane < nh, cs_f, jnp.where(lane < 2 * nh, cs_b, jnp.log(dt))) * LOG2E
    pt2 = p2.T
    sc = _split_hi_lo(jnp.exp2(p2))
    sc_f = jnp.dot(sc, e_ref[0], preferred_element_type=F32)
    sc_b = jnp.dot(sc, e_ref[1], preferred_element_type=F32)
    hp_per_g = nh // 2 // SSD_GROUPS

    for g in range(SSD_GROUPS):
        bg = x_ref[:, SSD_INNER + g * SSD_STATE:SSD_INNER + (g + 1) * SSD_STATE]
        cg = x_ref[:, SSD_INNER + (SSD_GROUPS + g) * SSD_STATE:SSD_INNER + (SSD_GROUPS + g + 1) * SSD_STATE]
        cb = lax.dot_general(cg, bg, (((1,), (1,)), ((), ())), preferred_element_type=F32)
        gsl = slice(g * SSD_GROUP_W, (g + 1) * SSD_GROUP_W)
        yoff_f = jnp.dot(cg, sef_ref[:, gsl], preferred_element_type=F32)
        yoff_b = jnp.dot(cg, seb_ref[:, gsl], preferred_element_type=F32)
        dcb = jnp.sum(cg.astype(F32) * bg.astype(F32), axis=1, keepdims=True)
        dself = jnp.dot(_split_hi_lo(dcb * dt + dsk_ref[...]), e_ref[2, :, gsl], preferred_element_type=F32)
        for jj in range(hp_per_g):
            j = g * hp_per_g + jj
            sl = slice(j * LANES, (j + 1) * LANES)
            xp = x_ref[:, sl]
            res = []
            for h in (2 * j, 2 * j + 1):
                r_f = pt2[h:h + 1, :] - pt2[2 * nh + h:2 * nh + h + 1, :]
                r_b = pt2[nh + h:nh + h + 1, :] - pt2[3 * nh + h:3 * nh + h + 1, :]
                arg = jnp.where(low, p2[:, h:h + 1] - r_f, p2[:, nh + h:nh + h + 1] - r_b)
                m = (cb * jnp.exp2(arg)).astype(BF16)
                res.append(jnp.dot(m, xp, preferred_element_type=F32))
            y_scr[:, sl] = jnp.where(lane_lo, res[0], res[1])
        o_ref[:, gsl] = (y_scr[:, gsl] + sc_f[:, gsl] * yoff_f + sc_b[:, gsl] * yoff_b
                         + dself * x_ref[:, gsl].astype(F32)).astype(BF16)


def _ssdy_kernel(x_ref, dt_ref, a_ref, dsk_ref, e_ref, sef_ref, seb_ref, o_ref, y_scr):
    for sub in range(CPS):
        rows = pl.ds(sub * Q, Q)
        _ssdy_chunk(x_ref.at[rows], dt_ref.at[rows], a_ref, dsk_ref, e_ref, sef_ref.at[sub], seb_ref.at[sub],
                    o_ref.at[rows], y_scr.at[sub])


def _ssdy_call(xact, dtsp, a128, dskip, sef, seb):
    b, l, _ = xact.shape
    ns = l // (CPS * Q)
    qs = CPS * Q
    return pl.pallas_call(
        _ssdy_kernel,
        grid=(b, ns),
        in_specs=[pl.BlockSpec((None, qs, SSD_CONV_DIM), lambda bi, ci: (bi, ci, OFF_XBC // SSD_CONV_DIM)),
                  pl.BlockSpec((None, qs, DT_W), lambda bi, ci: (bi, ci, 0)),
                  pl.BlockSpec((1, DT_W), lambda bi, ci: (0, 0)),
                  pl.BlockSpec((1, DT_W), lambda bi, ci: (0, 0)),
                  pl.BlockSpec((3, 2 * DT_W, SSD_INNER), lambda bi, ci: (0, 0, 0)),
                  pl.BlockSpec((None, CPS, SSD_STATE, SSD_INNER), lambda bi, ci: (bi, ci, 0, 0)),
                  pl.BlockSpec((None, CPS, SSD_STATE, SSD_INNER), lambda bi, ci: (bi, ci, 0, 0))],
        out_specs=pl.BlockSpec((None, qs, SSD_INNER), lambda bi, ci: (bi, ci, 0)),
        out_shape=jax.ShapeDtypeStruct((b, l, SSD_INNER), BF16),
        scratch_shapes=[pltpu.VMEM((CPS, Q, SSD_INNER), F32)],
        compiler_params=_cparams(("arbitrary", "arbitrary")),
        name="ssd_y",
    )(xact, dtsp, a128, dskip, _head_expand_matrices((0, SSD_HEADS, 3 * SSD_HEADS)), sef, seb)


def _merge_kernel(ya_ref, lg_ref, y_ref, z_ref, gt_ref, x_ref, mod_ref, snw_ref, wl_ref, ws_ref, wo_ref, o_ref):
    ra_in = (ya_ref[...].astype(F32) * _gelu_tanh(lg_ref[...].astype(F32))).astype(BF16)
    ra = jnp.dot(ra_in, wl_ref[...], preferred_element_type=F32)
    gn = []
    for g in range(SSD_GROUPS):
        gsl = slice(g * SSD_GROUP_W, (g + 1) * SSD_GROUP_W)
        zg = z_ref[:, gsl].astype(F32)
        gated = y_ref[:, gsl].astype(F32) * (zg * _sigmoid_tanh(zg))
        ms = jnp.mean(gated * gated, axis=-1, keepdims=True)
        gn.append((gated * lax.rsqrt(ms + EPS) * snw_ref[:, gsl]).astype(BF16))
    rb = jnp.dot(jnp.concatenate(gn, axis=1), ws_ref[...], preferred_element_type=F32)
    ga = _sigmoid_tanh(gt_ref[:, :D_MODEL].astype(F32))
    gb = _sigmoid_tanh(gt_ref[:, D_MODEL:].astype(F32))
    m = (ga * ra + gb * rb).astype(BF16)
    o = jnp.dot(m, wo_ref[...], preferred_element_type=F32)
    o_ref[...] = x_ref[...] + mod_ref[2:3, :] * o


def _merge_call(ya, y, big, x, mod, snw, wl, ws, wo):
    b, l, d = x.shape
    tm = min(512, l)
    const = lambda bi, i: (0, 0)
    single = pl.Buffered(1)
    return pl.pallas_call(
        _merge_kernel,
        grid=(b, l // tm),
        in_specs=[pl.BlockSpec((None, tm, LRU_WIDTH), lambda bi, i: (bi, i, 0)),
                  pl.BlockSpec((None, tm, LRU_WIDTH), lambda bi, i: (bi, i, OFF_LG // LRU_WIDTH)),
                  pl.BlockSpec((None, tm, SSD_INNER), lambda bi, i: (bi, i, 0)),
                  pl.BlockSpec((None, tm, SSD_INNER), lambda bi, i: (bi, i, OFF_Z // SSD_INNER)),
                  pl.BlockSpec((None, tm, 2 * d), lambda bi, i: (bi, i, OFF_GT // (2 * D_MODEL))),
                  pl.BlockSpec((None, tm, d), lambda bi, i: (bi, i, 0)),
                  pl.BlockSpec((None, 6, d), lambda bi, i: (bi, 0, 0)),
                  pl.BlockSpec((1, SSD_INNER), const),
                  pl.BlockSpec((LRU_WIDTH, d), const, pipeline_mode=single),
                  pl.BlockSpec((SSD_INNER, d), const, pipeline_mode=single),
                  pl.BlockSpec((d, d), const, pipeline_mode=single)],
        out_specs=pl.BlockSpec((None, tm, d), lambda bi, i: (bi, i, 0)),
        out_shape=jax.ShapeDtypeStruct((b, l, d), F32),
        compiler_params=_cparams(("arbitrary", "arbitrary")),
        name="merge",
    )(ya, big, y, big, big, x, mod, snw, wl, ws, wo)


def _ffn_kernel(*refs, tm, gw, halo, final_norm):
    if halo:
        (x_ref, xp_ref, xn_ref, nw_ref, mod_ref, wu_ref, cw_ref, cb_ref, wd_ref, fnw_ref,
         o_ref, h_scr, u_scr, v_scr, act_scr) = refs
    else:
        (x_ref, nw_ref, mod_ref, wu_ref, cw_ref, cb_ref, wd_ref, fnw_ref,
         o_ref, h_scr, u_scr, v_scr, act_scr) = refs
    i = pl.program_id(1)
    n_i = pl.num_programs(1)
    nw = nw_ref[...]
    shift = mod_ref[3:4, :]
    scale = mod_ref[4:5, :]
    ext = tm + 2 * gw if halo else tm
    top = gw if halo else 0
    pad = SUBLANES

    x = x_ref[...]
    h_scr[top:top + tm, :] = _norm_mod(x, nw, shift, scale).astype(BF16)
    if halo:
        hp = _norm_mod(xp_ref[...], nw, shift, scale)
        hn = _norm_mod(xn_ref[...], nw, shift, scale)
        h_scr[0:gw, :] = jnp.where(i > 0, hp, 0.0).astype(BF16)
        h_scr[gw + tm:ext, :] = jnp.where(i < n_i - 1, hn, 0.0).astype(BF16)
    for slot in range(2):
        u_scr[slot, 0:pad, :] = jnp.zeros((pad, FCH), F32)
        u_scr[slot, pad + ext:pad + ext + pad, :] = jnp.zeros((pad, FCH), F32)

    def up(ci, slot):
        c0 = pl.multiple_of(ci * FCH, FCH)
        u_scr[slot, pad:pad + ext, :] = jnp.dot(h_scr[...], wu_ref[:, pl.ds(c0, FCH)],
                                                preferred_element_type=F32)
        v_scr[slot] = jnp.dot(h_scr[top:top + tm, :], wu_ref[:, pl.ds(FFN_DIM + c0, FCH)],
                              preferred_element_type=F32)

    def down(ci, slot):
        c0 = pl.multiple_of(ci * FCH, FCH)
        colid = lax.broadcasted_iota(jnp.int32, (ext, FCH), 0) & (gw - 1)
        u = u_scr[slot, pad:pad + ext, :]
        ul = jnp.where(colid >= 1, u_scr[slot, pad - 1:pad - 1 + ext, :], 0.0)
        ur = jnp.where(colid <= gw - 2, u_scr[slot, pad + 1:pad + 1 + ext, :], 0.0)
        cw = cw_ref[:, pl.ds(c0, FCH)]
        acc = cb_ref[:, pl.ds(c0, FCH)]
        for dr in ((-1, 0, 1) if halo else (0,)):
            o = top + dr * gw
            k = (dr + 1) * 3
            acc = (acc + cw[k:k + 1, :] * ul[o:o + tm] + cw[k + 1:k + 2, :] * u[o:o + tm]
                   + cw[k + 2:k + 3, :] * ur[o:o + tm])
        act_scr[:, pl.ds(c0, FCH)] = (_gelu_tanh(acc) * v_scr[slot]).astype(BF16)

    n_ch = FFN_DIM // FCH
    up(0, 0)
    acc = None
    k0 = 0
    for c in range(n_ch):
        if c + 1 < n_ch:
            up(c + 1, (c + 1) % 2)
        down(c, c % 2)
        if (c + 1) in DOWN_SPLITS:
            k1 = (c + 1) * FCH
            part = jnp.dot(act_scr[:, k0:k1], wd_ref[k0:k1, :], preferred_element_type=F32)
            acc = part if acc is None else acc + part
            k0 = k1
    y = x + mod_ref[5:6, :] * acc
    if final_norm:
        ms = jnp.mean(y * y, axis=-1, keepdims=True)
        y = y * lax.rsqrt(ms + EPS) * fnw_ref[...]
    o_ref[...] = y


def _ffn_call(x, nw, mod, wu, cw, cb, wd, fnw, *, gw, halo, final_norm):
    b, l, d = x.shape
    tm = min(1024, l)
    assert tm % gw == 0 and gw & (gw - 1) == 0
    ext = tm + 2 * gw if halo else tm
    r = tm // gw
    ng = l // gw
    const = lambda bi, i: (0, 0)
    single = pl.Buffered(1)
    in_specs = [pl.BlockSpec((None, tm, d), lambda bi, i: (bi, i, 0))]
    args = [x]
    if halo:
        in_specs += [pl.BlockSpec((None, gw, d), lambda bi, i: (bi, jnp.maximum(i * r - 1, 0), 0)),
                     pl.BlockSpec((None, gw, d), lambda bi, i: (bi, jnp.minimum((i + 1) * r, ng - 1), 0))]
        args += [x, x]
    in_specs += [pl.BlockSpec((1, d), const),
                 pl.BlockSpec((None, 6, d), lambda bi, i: (bi, 0, 0)),
                 pl.BlockSpec((d, 2 * FFN_DIM), const, pipeline_mode=single),
                 pl.BlockSpec((9, FFN_DIM), const),
                 pl.BlockSpec((1, FFN_DIM), const),
                 pl.BlockSpec((FFN_DIM, d), const, pipeline_mode=single),
                 pl.BlockSpec((1, d), const)]
    args += [nw, mod, wu, cw, cb, wd, fnw]
    kern = functools.partial(_ffn_kernel, tm=tm, gw=gw, halo=halo, final_norm=final_norm)
    return pl.pallas_call(
        kern,
        grid=(b, l // tm),
        in_specs=in_specs,
        out_specs=pl.BlockSpec((None, tm, d), lambda bi, i: (bi, i, 0)),
        out_shape=jax.ShapeDtypeStruct((b, l, d), F32),
        scratch_shapes=[pltpu.VMEM((ext, d), BF16),
                        pltpu.VMEM((2, ext + 2 * SUBLANES, FCH), F32),
                        pltpu.VMEM((2, tm, FCH), F32),
                        pltpu.VMEM((tm, FFN_DIM), BF16)],
        compiler_params=_cparams(("arbitrary", "arbitrary")),
        name="ffn",
    )(*args)


def _gate_weights(wa, wx):
    hpb = CBLK // LRU_HEAD_DIM
    nblk = LRU_WIDTH // CBLK
    eye = jnp.eye(hpb, dtype=F32)

    def blockdiag(w):
        w = w.reshape(nblk, hpb, LRU_HEAD_DIM, LRU_HEAD_DIM)
        bd = jnp.einsum('bhij,hk->bhikj', w, eye)
        return bd.reshape(nblk, CBLK, CBLK)

    parts = [blockdiag(wa[0]), blockdiag(wx[0]), blockdiag(wa[1]), blockdiag(wx[1])]
    return jnp.concatenate(parts, axis=-1).astype(BF16)


def _gate_biases(ba, bx):
    nblk = LRU_WIDTH // CBLK
    parts = [v.reshape(nblk, 1, CBLK) for v in (ba[0], bx[0], ba[1], bx[1])]
    return jnp.concatenate(parts, axis=-1)


def kernel(x, c, ctx, c_ctx, ada_w, ada_b, norm_mix_w, norm_ffn_w, w_in, lru_conv_w, lru_conv_b, lru_wa, lru_ba, lru_wx, lru_bx, lru_lambda, lru_proj, ssd_conv_w, ssd_conv_b, ssd_dt_bias, ssd_a_log, ssd_d, ssd_norm_w, ssd_proj, w_out, ffn_w_up, ffn_conv_w, ffn_conv_b, ffn_w_down, final_norm_w):
    b, l, d = x.shape
    lc = ctx.shape[1]
    depth = ada_w.shape[0]
    gw_x = 64
    assert d == D_MODEL and b + 1 <= 16 and l % (CPS * Q) == 0 and lc % (CPS * Q) == 0

    s = jnp.zeros((16, d), F32).at[:b].set(c).at[b].set(c_ctx)
    mods = _ada_call(s, ada_w, ada_b)

    o_lx, o_lg, o_z, o_xbc, o_dt, o_gt = 0, 1024, 2048, 4096, 7168, 7232
    h0 = jnp.zeros((b, 2, LRU_WIDTH), F32)
    s0 = jnp.zeros((b, 2, SSD_STATE, SSD_INNER), F32)
    fnw = final_norm_w.reshape(1, d)
    ctxf = ctx.reshape(1, b * lc, d)

    for li in range(depth):
        last = li == depth - 1
        modx = mods[li, :b].reshape(b, 6, d)
        modc = mods[li, b].reshape(1, 6, d)
        wl = w_in[li]
        w_big = jnp.concatenate([wl[:, o_z:o_xbc], wl[:, o_gt:], wl[:, o_lx:o_z], wl[:, o_xbc:o_dt]],
                                axis=1).astype(BF16)
        w_dt = jnp.concatenate([wl[:, o_dt:o_gt], wl[:, o_dt:o_gt]], axis=1).astype(BF16)
        nmw = norm_mix_w[li].reshape(1, d)
        nfw = norm_ffn_w[li].reshape(1, d)
        wg = _gate_weights(lru_wa[li], lru_wx[li])
        bgs = _gate_biases(lru_ba[li], lru_bx[li])
        lcw, lcb = lru_conv_w[li], lru_conv_b[li].reshape(1, -1)
        scw, scb = ssd_conv_w[li], ssd_conv_b[li].reshape(1, -1)
        dtb = jnp.tile(ssd_dt_bias[li].reshape(1, -1), (1, 2))
        a128 = jnp.tile(-jnp.exp(ssd_a_log[li].astype(F32)).reshape(1, -1), (1, 2))
        dsk = jnp.concatenate([jnp.zeros((1, DT_W - SSD_HEADS), F32), ssd_d[li].reshape(1, -1)], axis=1)
        snw = ssd_norm_w[li].reshape(1, -1)
        wlp = lru_proj[li].astype(BF16)
        wsp = ssd_proj[li].astype(BF16)
        wo = w_out[li].astype(BF16)
        wu = ffn_w_up[li].astype(BF16)
        wd = ffn_w_down[li].astype(BF16)
        fcw = ffn_conv_w[li].reshape(9, FFN_DIM)
        fcb = ffn_conv_b[li].reshape(1, FFN_DIM)

        big_cf, dtsp_cf = _inproj_call(ctxf, nmw, modc, w_big, w_dt, dtb, scw, scb, seq=lc)
        big_c = xact_c = big_cf.reshape(b, lc, BIG_W)
        dtsp_c = dtsp_cf.reshape(b, lc, DT_W)
        ra_c, hfin_c = _lru_call(big_c, lcw, lcb, wg, bgs, lru_lambda[li], h0)
        sef_c, seb_c, sfin_c = _ssdstate_call(xact_c, dtsp_c, a128, s0)

        big_x, dtsp_x = _inproj_call(x, nmw, modx, w_big, w_dt, dtb, scw, scb, seq=l)
        xact_x = big_x
        ra_x, _ = _lru_call(big_x, lcw, lcb, wg, bgs, lru_lambda[li], hfin_c)
        sef_x, seb_x, _ = _ssdstate_call(xact_x, dtsp_x, a128, sfin_c)
        y_x = _ssdy_call(xact_x, dtsp_x, a128, dsk, sef_x, seb_x)
        x = _merge_call(ra_x, y_x, big_x, x, modx, snw, wlp, wsp, wo)
        x = _ffn_call(x, nfw, modx, wu, fcw, fcb, wd, fnw, gw=gw_x, halo=True, final_norm=last)

        if not last:
            y_c = _ssdy_call(xact_c, dtsp_c, a128, dsk, sef_c, seb_c)
            ctxf = _merge_call(ra_c.reshape(1, b * lc, LRU_WIDTH), y_c.reshape(1, b * lc, SSD_INNER),
                               big_cf, ctxf, modc, snw, wlp, wsp, wo)
            ctxf = _ffn_call(ctxf, nfw, modc, wu, fcw, fcb, wd, fnw, gw=lc, halo=False, final_norm=False)
    return x
```
